```python
import jax, jax.numpy as jnp
from jax import lax
import numpy as np

D_MODEL = 1024
BATCH = 4
SEQ = 8192
DEPTH = 4

N_BRANCH = 4
BRANCH_WIDTH = 512
ML_HEADS = 4
ML_DK = 128
ML_DV = 128
FX_HEADS = 4
FX_DH = 128
LRU_WIDTH = 512
LRU_BLOCKS = 8
LRU_BLOCK = 64
CONV_WIDTH = 4
LRU_C = 8.0
GLA_HEADS = 4
GLA_DK = 64
GLA_DV = 128
GLA_RANK = 16
GLA_TAU = 16.0
D_FF = 4 * D_MODEL
CHUNK = 64
Q_BLOCK = 128
EPS = 1e-6

IN_WIDTHS = (
    ML_HEADS * ML_DK, ML_HEADS * ML_DK, ML_HEADS * ML_DV, ML_HEADS * ML_DV, ML_HEADS, ML_HEADS,
    FX_HEADS * FX_DH, FX_HEADS * FX_DH, FX_HEADS * FX_DH, FX_HEADS,
    LRU_WIDTH, LRU_WIDTH,
    GLA_HEADS * GLA_DK, GLA_HEADS * GLA_DK, GLA_HEADS * GLA_DV, GLA_HEADS * GLA_DV, GLA_RANK,
    N_BRANCH * D_MODEL,
)
N_IN = sum(IN_WIDTHS)
ML_F_COL = 5
FX_F_COL = 9

kernel_name = "hybrid_mlstm_fox_rglru_gla_block"


def rms_norm(x, g):
    xf = x.astype(jnp.float32)
    y = xf * lax.rsqrt(jnp.mean(xf * xf, axis=-1, keepdims=True) + EPS)
    return (y * g.astype(jnp.float32)).astype(x.dtype)


def split_cols(z):
    offs = []
    acc = 0
    for w in IN_WIDTHS[:-1]:
        acc += w
        offs.append(acc)
    return jnp.split(z, offs, axis=-1)


def mlstm(q, k, v, i_pre, f_pre):
    f32 = jnp.float32
    B, S, H, dk = q.shape
    dv = v.shape[-1]
    nc = S // CHUNK

    def to_chunks(t):
        t = t.astype(f32).reshape((B, nc, CHUNK) + t.shape[2:])
        return jnp.swapaxes(t, 2, 3)

    qc = to_chunks(q) * (dk ** -0.5)
    kc = to_chunks(k)
    vc = to_chunks(v)
    log_i = to_chunks(i_pre)
    log_f = jax.nn.log_sigmoid(to_chunks(f_pre))
    b = jnp.cumsum(log_f, axis=-1)
    b_last = b[..., -1]

    w_end = b_last[..., None] - b + log_i
    m_loc = jnp.max(w_end, axis=-1)
    e_end = jnp.exp(w_end - m_loc[..., None])
    c_loc = jnp.einsum('bnhsk,bnhsv->bnhkv', kc * e_end[..., None], vc)
    n_loc = jnp.einsum('bnhs,bnhsk->bnhk', e_end, kc)

    def step(carry, inp):
        c, n, m = carry
        g, ml, cl, nl = inp
        m_new = jnp.maximum(g + m, ml)
        a = jnp.exp(g + m - m_new)
        bb = jnp.exp(ml - m_new)
        c_new = a[..., None, None] * c + bb[..., None, None] * cl
        n_new = a[..., None] * n + bb[..., None] * nl
        return (c_new, n_new, m_new), (c, n, m)

    init = (jnp.zeros((B, H, dk, dv), f32), jnp.zeros((B, H, dk), f32), jnp.zeros((B, H), f32))
    xs = (jnp.moveaxis(b_last, 1, 0), jnp.moveaxis(m_loc, 1, 0),
          jnp.moveaxis(c_loc, 1, 0), jnp.moveaxis(n_loc, 1, 0))
    _, (c_prev, n_prev, m_prev) = lax.scan(step, init, xs)
    c_prev = jnp.moveaxis(c_prev, 0, 1)
    n_prev = jnp.moveaxis(n_prev, 0, 1)
    m_prev = jnp.moveaxis(m_prev, 0, 1)

    causal = jnp.tril(jnp.ones((CHUNK, CHUNK), dtype=bool))
    d_ts = jnp.where(causal, b[..., :, None] - b[..., None, :] + log_i[..., None, :], -jnp.inf)
    inter_log = b + m_prev[..., None]
    m_t = jnp.maximum(inter_log, jnp.max(d_ts, axis=-1))
    s_qk = jnp.einsum('bnhtk,bnhsk->bnhts', qc, kc) * jnp.exp(d_ts - m_t[..., None])
    w_inter = jnp.exp(inter_log - m_t)
    num = (jnp.einsum('bnhts,bnhsv->bnhtv', s_qk, vc)
           + w_inter[..., None] * jnp.einsum('bnhtk,bnhkv->bnhtv', qc, c_prev))
    den = s_qk.sum(-1) + w_inter * jnp.einsum('bnhtk,bnhk->bnht', qc, n_prev)
    h = num / jnp.maximum(jnp.abs(den), jnp.exp(-m_t))[..., None]
    return jnp.swapaxes(h, 2, 3).reshape(B, S, H, dv)


def forgetting_attention(q, k, v, f_pre, gq, gk):
    f32 = jnp.float32
    B, S, H, Dh = q.shape
    q = rms_norm(q.astype(f32), gq) * (Dh ** -0.5)
    k = rms_norm(k.astype(f32), gk)
    v = v.astype(f32)
    F = jnp.cumsum(jax.nn.log_sigmoid(f_pre.astype(f32)), axis=1)
    Fk = jnp.transpose(F, (0, 2, 1))
    nb = S // Q_BLOCK
    qb = jnp.moveaxis(q.reshape(B, nb, Q_BLOCK, H, Dh), 1, 0)
    Fq = jnp.moveaxis(Fk.reshape(B, H, nb, Q_BLOCK), 2, 0)
    pos_k = jnp.arange(S)

    def block(args):
        qi, Fi, idx = args
        pos_q = idx * Q_BLOCK + jnp.arange(Q_BLOCK)
        s = jnp.einsum('bqhd,bkhd->bhqk', qi, k) + (Fi[..., :, None] - Fk[:, :, None, :])
        s = jnp.where(pos_k[None, :] <= pos_q[:, None], s, -jnp.inf)
        p = jax.nn.softmax(s, axis=-1)
        return jnp.einsum('bhqk,bkhd->bqhd', p, v)

    out = lax.map(block, (qb, Fq, jnp.arange(nb)))
    return jnp.moveaxis(out, 0, 1).reshape(B, S, H, Dh)


def rg_lru_branch(xb, yb, conv_w, conv_b, wa, ba, wx, bx, lam):
    f32 = jnp.float32
    B, S, W = xb.shape
    xf = xb.astype(f32)
    xc = lax.conv_general_dilated(xf, conv_w.astype(f32)[:, None, :], window_strides=(1,),
                                  padding=[(CONV_WIDTH - 1, 0)],
                                  dimension_numbers=('NWC', 'WIO', 'NWC'),
                                  feature_group_count=W) + conv_b.astype(f32)
    xg = xc.reshape(B, S, LRU_BLOCKS, LRU_BLOCK)
    r_gate = jax.nn.sigmoid(jnp.einsum('bsgi,gij->bsgj', xg, wa.astype(f32)).reshape(B, S, W) + ba.astype(f32))
    i_gate = jax.nn.sigmoid(jnp.einsum('bsgi,gij->bsgj', xg, wx.astype(f32)).reshape(B, S, W) + bx.astype(f32))
    log_a = -LRU_C * r_gate * jax.nn.softplus(-lam.astype(f32))
    a = jnp.exp(log_a)
    u = jnp.sqrt(-jnp.expm1(2.0 * log_a)) * (i_gate * xc)

    def combine(left, right):
        a1, b1 = left
        a2, b2 = right
        return a1 * a2, a2 * b1 + b2

    _, h = lax.associative_scan(combine, (a, u), axis=1)
    return h * jax.nn.gelu(yb.astype(f32))


def gla(q, k, v, log_alpha):
    f32 = jnp.float32
    B, S, H, dk = q.shape
    dv = v.shape[-1]
    nc = S // CHUNK

    def to_chunks(t):
        t = t.astype(f32).reshape((B, nc, CHUNK) + t.shape[2:])
        return jnp.swapaxes(t, 2, 3)

    qc = to_chunks(q) * (dk ** -0.5)
    kc = to_chunks(k)
    vc = to_chunks(v)
    bc = jnp.cumsum(to_chunks(log_alpha), axis=3)
    b_last = bc[..., -1, :]
    q_t = qc * jnp.exp(bc)
    k_t = kc * jnp.exp(-bc)
    causal = jnp.tril(jnp.ones((CHUNK, CHUNK), dtype=bool))
    attn = jnp.where(causal, jnp.einsum('bnhtk,bnhsk->bnhts', q_t, k_t), 0.0)
    o_intra = jnp.einsum('bnhts,bnhsv->bnhtv', attn, vc)
    ds = jnp.einsum('bnhsk,bnhsv->bnhkv', kc * jnp.exp(b_last[..., None, :] - bc), vc)

    def step(s, inp):
        decay, d = inp
        return decay[..., None] * s + d, s

    init = jnp.zeros((B, H, dk, dv), f32)
    _, s_prev = lax.scan(step, init, (jnp.moveaxis(jnp.exp(b_last), 1, 0), jnp.moveaxis(ds, 1, 0)))
    s_prev = jnp.moveaxis(s_prev, 0, 1)
    o = o_intra + jnp.einsum('bnhtk,bnhkv->bnhtv', q_t, s_prev)
    return jnp.swapaxes(o, 2, 3).reshape(B, S, H, dv)


def hybrid_layer(x, norm1_g, w_in, b_in, ml_norm_g, fx_qnorm_g, fx_knorm_g,
                 lru_conv_w, lru_conv_b, lru_wa, lru_ba, lru_wx, lru_bx, lru_lambda,
                 gla_w_alpha, gla_b_alpha, gla_norm_g, w_branch, w_o, norm2_g, w_up, w_down):
    f32 = jnp.float32
    B, S, _ = x.shape
    h = rms_norm(x, norm1_g)
    z = h @ w_in + b_in
    (mq, mk, mv, mo, mi, mf, fq, fk, fv, ff, rx, ry,
     gq, gk, gv, gg, ga, gates) = split_cols(z)

    y_ml = mlstm(mq.reshape(B, S, ML_HEADS, ML_DK), mk.reshape(B, S, ML_HEADS, ML_DK),
                 mv.reshape(B, S, ML_HEADS, ML_DV), mi, mf)
    y_ml = rms_norm(y_ml, ml_norm_g.reshape(ML_HEADS, ML_DV)).reshape(B, S, BRANCH_WIDTH)
    y_ml = y_ml * jax.nn.sigmoid(mo.astype(f32))

    y_fx = forgetting_attention(fq.reshape(B, S, FX_HEADS, FX_DH), fk.reshape(B, S, FX_HEADS, FX_DH),
                                fv.reshape(B, S, FX_HEADS, FX_DH), ff, fx_qnorm_g, fx_knorm_g)
    y_fx = y_fx.reshape(B, S, BRANCH_WIDTH)

    y_lru = rg_lru_branch(rx, ry, lru_conv_w, lru_conv_b, lru_wa, lru_ba, lru_wx, lru_bx, lru_lambda)

    log_alpha = jax.nn.log_sigmoid(ga.astype(f32) @ gla_w_alpha.astype(f32) + gla_b_alpha.astype(f32)) / GLA_TAU
    y_gla = gla(gq.reshape(B, S, GLA_HEADS, GLA_DK), gk.reshape(B, S, GLA_HEADS, GLA_DK),
                gv.reshape(B, S, GLA_HEADS, GLA_DV), log_alpha.reshape(B, S, GLA_HEADS, GLA_DK))
    y_gla = rms_norm(y_gla, gla_norm_g.reshape(GLA_HEADS, GLA_DV)).reshape(B, S, BRANCH_WIDTH)
    y_gla = y_gla * jax.nn.silu(gg.astype(f32))

    ycat = jnp.stack([y_ml, y_fx, y_lru, y_gla], axis=2).astype(x.dtype)
    yb = jnp.einsum('bsnw,nwd->bsnd', ycat, w_branch)
    g = jax.nn.sigmoid(gates.reshape(B, S, N_BRANCH, D_MODEL))
    merged = jnp.sum(g * yb, axis=2)
    x = x + merged @ w_o

    h2 = rms_norm(x, norm2_g)
    x = x + jnp.square(jax.nn.relu(h2 @ w_up)) @ w_down
    return x


def setup_inputs(seed: int = 0) -> dict:
    key = jax.random.key(seed)
    ks = jax.random.split(key, 24)
    f32 = jnp.float32

    def nrm(k, shape, scale):
        return jax.random.normal(k, shape, f32) * scale

    x = nrm(ks[0], (BATCH, SEQ, D_MODEL), 1.0)
    norm1_g = 1.0 + nrm(ks[1], (DEPTH, D_MODEL), 0.02)
    w_in = nrm(ks[2], (DEPTH, D_MODEL, N_IN), D_MODEL ** -0.5)
    b_in = nrm(ks[3], (DEPTH, N_IN), 0.02)
    ml_f_off = sum(IN_WIDTHS[:ML_F_COL])
    fx_f_off = sum(IN_WIDTHS[:FX_F_COL])
    b_in = b_in.at[:, ml_f_off:ml_f_off + ML_HEADS].add(jnp.linspace(3.0, 6.0, ML_HEADS, dtype=f32))
    b_in = b_in.at[:, fx_f_off:fx_f_off + FX_HEADS].add(jnp.linspace(3.0, 6.0, FX_HEADS, dtype=f32))
    ml_norm_g = 1.0 + nrm(ks[4], (DEPTH, ML_HEADS * ML_DV), 0.02)
    fx_qnorm_g = 1.0 + nrm(ks[5], (DEPTH, FX_DH), 0.02)
    fx_knorm_g = 1.0 + nrm(ks[6], (DEPTH, FX_DH), 0.02)
    lru_conv_w = nrm(ks[7], (DEPTH, CONV_WIDTH, LRU_WIDTH), CONV_WIDTH ** -0.5)
    lru_conv_b = nrm(ks[8], (DEPTH, LRU_WIDTH), 0.02)
    lru_wa = nrm(ks[9], (DEPTH, LRU_BLOCKS, LRU_BLOCK, LRU_BLOCK), LRU_BLOCK ** -0.5)
    lru_ba = nrm(ks[10], (DEPTH, LRU_WIDTH), 0.02)
    lru_wx = nrm(ks[11], (DEPTH, LRU_BLOCKS, LRU_BLOCK, LRU_BLOCK), LRU_BLOCK ** -0.5)
    lru_bx = nrm(ks[12], (DEPTH, LRU_WIDTH), 0.02)
    u = jax.random.uniform(ks[13], (DEPTH, LRU_WIDTH), f32, 0.9, 0.999)
    a0 = u ** (1.0 / LRU_C)
    lru_lambda = jnp.log(a0) - jnp.log1p(-a0)
    gla_w_alpha = nrm(ks[14], (DEPTH, GLA_RANK, GLA_HEADS * GLA_DK), GLA_RANK ** -0.5)
    gla_b_alpha = nrm(ks[15], (DEPTH, GLA_HEADS * GLA_DK), 0.02)
    gla_norm_g = 1.0 + nrm(ks[16], (DEPTH, GLA_HEADS * GLA_DV), 0.02)
    w_branch = nrm(ks[17], (DEPTH, N_BRANCH, BRANCH_WIDTH, D_MODEL), BRANCH_WIDTH ** -0.5)
    w_o = nrm(ks[18], (DEPTH, D_MODEL, D_MODEL), D_MODEL ** -0.5)
    norm2_g = 1.0 + nrm(ks[19], (DEPTH, D_MODEL), 0.02)
    w_up = nrm(ks[20], (DEPTH, D_MODEL, D_FF), D_MODEL ** -0.5)
    w_down = nrm(ks[21], (DEPTH, D_FF, D_MODEL), D_FF ** -0.5)
    return {"x": x, "norm1_g": norm1_g, "w_in": w_in, "b_in": b_in, "ml_norm_g": ml_norm_g,
            "fx_qnorm_g": fx_qnorm_g, "fx_knorm_g": fx_knorm_g, "lru_conv_w": lru_conv_w,
            "lru_conv_b": lru_conv_b, "lru_wa": lru_wa, "lru_ba": lru_ba, "lru_wx": lru_wx,
            "lru_bx": lru_bx, "lru_lambda": lru_lambda, "gla_w_alpha": gla_w_alpha,
            "gla_b_alpha": gla_b_alpha, "gla_norm_g": gla_norm_g, "w_branch": w_branch, "w_o": w_o,
            "norm2_g": norm2_g, "w_up": w_up, "w_down": w_down}


def reference(x, norm1_g, w_in, b_in, ml_norm_g, fx_qnorm_g, fx_knorm_g, lru_conv_w, lru_conv_b,
              lru_wa, lru_ba, lru_wx, lru_bx, lru_lambda, gla_w_alpha, gla_b_alpha, gla_norm_g,
              w_branch, w_o, norm2_g, w_up, w_down):
    for l in range(DEPTH):
        x = hybrid_layer(x, norm1_g[l], w_in[l], b_in[l], ml_norm_g[l], fx_qnorm_g[l], fx_knorm_g[l],
                         lru_conv_w[l], lru_conv_b[l], lru_wa[l], lru_ba[l], lru_wx[l], lru_bx[l],
                         lru_lambda[l], gla_w_alpha[l], gla_b_alpha[l], gla_norm_g[l],
                         w_branch[l], w_o[l], norm2_g[l], w_up[l], w_down[l])
    return x
```

```python
import functools

import jax
import jax.numpy as jnp
from jax import lax
from jax.experimental import pallas as pl
from jax.experimental.pallas import tpu as pltpu

F32 = jnp.float32
BF16 = jnp.bfloat16
HIGHEST = lax.Precision.HIGHEST

N_BRANCH = 4
BRANCH_WIDTH = 512
HEADS = 4
ML_DK = 128
FX_DH = 128
HEAD_DV = 128
LRU_BLOCKS = 8
LRU_BLOCK = 64
CONV_WIDTH = 4
LRU_C = 8.0
GLA_DK = 64
GLA_RANK = 16
GLA_TAU = 16.0
GLA_CHUNK = 64
EPS = 1e-6

LANES = 128
SUBLANES = 8
VMEM_LIMIT = 56 * 1024 * 1024

SM_MI, SM_MF, SM_FF, SM_GA = 0, 4, 8, 12
SMALL_W = LANES
NEG = -1e30


def _cparams(sem):
    return pltpu.CompilerParams(dimension_semantics=sem, vmem_limit_bytes=VMEM_LIMIT)


def _log_sigmoid(x):
    return jnp.minimum(x, 0.0) - jnp.log1p(jnp.exp(-jnp.abs(x)))


def _softplus(x):
    return jnp.maximum(x, 0.0) + jnp.log1p(jnp.exp(-jnp.abs(x)))


def _sigmoid(x):
    return 1.0 / (1.0 + jnp.exp(-x))


def _rms(x, g):
    return x * lax.rsqrt(jnp.mean(x * x, axis=-1, keepdims=True) + EPS) * g


def _dot(a, b, **kw):
    return jnp.dot(a, b, preferred_element_type=F32, **kw)


def _dot_nt(a, b, **kw):
    return lax.dot_general(a, b, (((1,), (1,)), ((), ())), preferred_element_type=F32, **kw)


def _dot_tn(a, b, **kw):
    return lax.dot_general(a, b, (((0,), (0,)), ((), ())), preferred_element_type=F32, **kw)


G_MLQ, G_MLK, G_MLV, G_MLO, G_FXQ, G_FXK, G_FXV, G_RX, G_RY, G_GQK, G_GV, G_GG = range(12)
N_GROUPS = 12
BW = BRANCH_WIDTH
K1_OUT_DTYPES = (BF16, BF16, BF16, F32, BF16, BF16, BF16, F32, F32, F32, BF16, F32)


def _inproj_body(x_ref, g1_ref, w_ref, b_ref, gq_ref, gk_ref, cs_ref, *out_refs):
    x = x_ref[...]
    h = _rms(x, g1_ref[...]).astype(BF16)

    def proj(off, width):
        return _dot(h, w_ref[:, off:off + width]) + b_ref[:, off:off + width]

    for gi in range(N_GROUPS):
        z = proj(gi * BW, BW)
        o_ref = out_refs[gi]
        if gi in (G_FXQ, G_FXK):
            g = gq_ref[...] if gi == G_FXQ else gk_ref[...]
            scale = FX_DH ** -0.5 if gi == G_FXQ else 1.0
            for hd in range(HEADS):
                zh = z[:, hd * FX_DH:(hd + 1) * FX_DH]
                o_ref[:, hd * FX_DH:(hd + 1) * FX_DH] = (_rms(zh, g) * scale).astype(o_ref.dtype)
        elif gi in (G_MLQ, G_GQK):
            o_ref[...] = (z * cs_ref[:, gi * BW:(gi + 1) * BW]).astype(o_ref.dtype)
        else:
            o_ref[...] = z.astype(o_ref.dtype)
    out_refs[N_GROUPS][...] = proj(N_GROUPS * BW, SMALL_W)


def _inproj(x2, g1, w1, b1, gq, gk, cs, *, tm):
    T, D = x2.shape
    N1 = w1.shape[1]
    const = lambda i: (0, 0)
    row = lambda i: (i, 0)
    out_shape = [jax.ShapeDtypeStruct((T, BW), dt) for dt in K1_OUT_DTYPES]
    out_shape.append(jax.ShapeDtypeStruct((T, SMALL_W), F32))
    out_specs = [pl.BlockSpec((tm, BW), row) for _ in K1_OUT_DTYPES]
    out_specs.append(pl.BlockSpec((tm, SMALL_W), row))
    return pl.pallas_call(
        _inproj_body,
        grid=(T // tm,),
        in_specs=[
            pl.BlockSpec((tm, D), row),
            pl.BlockSpec((1, D), const),
            pl.BlockSpec((D, N1), const),
            pl.BlockSpec((1, N1), const),
            pl.BlockSpec((1, FX_DH), const),
            pl.BlockSpec((1, FX_DH), const),
            pl.BlockSpec((1, N1), const),
        ],
        out_specs=out_specs,
        out_shape=out_shape,
        compiler_params=_cparams(("parallel",)),
        name="inproj",
    )(x2, g1, w1, b1, gq, gk, cs)


def _fcum_body(f_ref, o_ref):
    ls = _log_sigmoid(f_ref[0])
    nb = ls.shape[0]
    ii = lax.broadcasted_iota(jnp.int32, (LANES, LANES), 0)
    jj = lax.broadcasted_iota(jnp.int32, (LANES, LANES), 1)
    local = _dot(ls, (ii <= jj).astype(F32), precision=HIGHEST)
    tot = jnp.broadcast_to(local[:, LANES - 1:LANES], (nb, LANES))
    ri = lax.broadcasted_iota(jnp.int32, (nb, nb), 0)
    ci = lax.broadcasted_iota(jnp.int32, (nb, nb), 1)
    offs = _dot((ci < ri).astype(F32), tot, precision=HIGHEST)
    o_ref[0] = local + offs


def _fcum(f3):
    G, nb, _ = f3.shape
    spec = pl.BlockSpec((1, nb, LANES), lambda i: (i, 0, 0))
    return pl.pallas_call(
        _fcum_body, grid=(G,), in_specs=[spec], out_specs=spec,
        out_shape=jax.ShapeDtypeStruct(f3.shape, F32),
        compiler_params=_cparams(("parallel",)), name="fox_cumsum",
    )(f3)


def _fox_body(q_ref, k_ref, v_ref, f_ref, o_ref, *, tq):
    qi = pl.program_id(2)
    q = q_ref[0]
    q0 = pl.multiple_of(qi * tq, tq)
    c = f_ref[0, 0, :, pl.ds(q0, LANES)][:, 0:1]

    def step(kj, carry, masked):
        m, l, acc = carry
        k0 = pl.multiple_of(kj * tq, tq)
        kb = k_ref[0, pl.ds(k0, tq), :]
        vb = v_ref[0, pl.ds(k0, tq), :]
        s = _dot_nt(q, kb) + (c - f_ref[0, 0, :, pl.ds(k0, tq)])
        if masked:
            row = lax.broadcasted_iota(jnp.int32, (tq, tq), 0)
            col = lax.broadcasted_iota(jnp.int32, (tq, tq), 1)
            s = jnp.where(col <= row, s, NEG)
        m_new = jnp.maximum(m, jnp.max(s, axis=-1, keepdims=True))
        alpha = jnp.exp(m - m_new)
        p = jnp.exp(s - m_new)
        l = alpha * l + jnp.sum(p, axis=-1, keepdims=True)
        acc = alpha * acc + _dot(p.astype(BF16), vb)
        return m_new, l, acc

    init = (jnp.full((tq, 1), NEG, F32), jnp.zeros((tq, 1), F32), jnp.zeros((tq, HEAD_DV), F32))
    carry = lax.fori_loop(0, qi, functools.partial(step, masked=False), init)
    _, l, acc = step(qi, carry, True)
    o_ref[0] = (acc / l).astype(o_ref.dtype)


def _fox(q, k, v, fcum, *, tq):
    B, S, _ = q.shape
    qspec = pl.BlockSpec((1, tq, FX_DH), lambda b, h, i: (b, i, h))
    kvspec = pl.BlockSpec((1, S, FX_DH), lambda b, h, i: (b, 0, h))
    fspec = pl.BlockSpec((1, 1, 1, S), lambda b, h, i: (b, h, 0, 0))
    return pl.pallas_call(
        functools.partial(_fox_body, tq=tq),
        grid=(B, HEADS, S // tq),
        in_specs=[qspec, kvspec, kvspec, fspec],
        out_specs=qspec,
        out_shape=jax.ShapeDtypeStruct((B, S, BW), BF16),
        compiler_params=_cparams(("parallel", "parallel", "arbitrary")),
        name="fox_attn",
    )(q, k, v, fcum)


def _mlstm_body(q_ref, k_ref, v_ref, sm_ref, o_ref, gn_ref, y_ref, c_ref, n_ref, m_ref, *, L):
    @pl.when(pl.program_id(1) == 0)
    def _():
        c_ref[...] = jnp.zeros_like(c_ref)
        n_ref[...] = jnp.zeros_like(n_ref)
        m_ref[...] = jnp.zeros_like(m_ref)

    g = sm_ref[0]
    gt = g.T
    lf = _log_sigmoid(g)
    lft = _log_sigmoid(gt[0:SUBLANES, :])
    ri = lax.broadcasted_iota(jnp.int32, (L, L), 0)
    ci = lax.broadcasted_iota(jnp.int32, (L, L), 1)
    causal = ci <= ri
    b_cols = _dot(causal.astype(F32), lf, precision=HIGHEST)
    b_rows = _dot(lft, (ri <= ci).astype(F32), precision=HIGHEST)

    for hd in range(HEADS):
        sl = slice(hd * HEAD_DV, (hd + 1) * HEAD_DV)
        q = q_ref[0, :, sl]
        k = k_ref[0, :, sl]
        v = v_ref[0, :, sl]
        b_col = b_cols[:, SM_MF + hd:SM_MF + hd + 1]
        b_row = b_rows[SM_MF + hd:SM_MF + hd + 1, :]
        li_col = g[:, SM_MI + hd:SM_MI + hd + 1]
        li_row = gt[SM_MI + hd:SM_MI + hd + 1, :]
        b_last = b_col[L - 1:L, :]
        c_prev = c_ref[hd]
        n_prev = n_ref[hd]
        m_prev = m_ref[hd][0:1, 0:1]

        d = jnp.where(causal, b_col - b_row + li_row, -jnp.inf)
        inter = b_col + m_prev
        m_t = jnp.maximum(inter, jnp.max(d, axis=-1, keepdims=True))
        s_qk = _dot_nt(q, k) * jnp.exp(d - m_t)
        w_inter = jnp.exp(inter - m_t)
        num = _dot(s_qk.astype(BF16), v) + w_inter * _dot(q, c_prev.astype(BF16))
        qn = jnp.sum(q.astype(F32) * n_prev, axis=-1, keepdims=True)
        den = jnp.sum(s_qk, axis=-1, keepdims=True) + w_inter * qn
        hh = num / jnp.maximum(jnp.abs(den), jnp.exp(-m_t))
        y = _rms(hh, gn_ref[:, sl]) * _sigmoid(o_ref[0, :, sl])
        y_ref[0, :, sl] = y.astype(y_ref.dtype)

        m_loc = jnp.max(b_last - b_row + li_row, axis=-1, keepdims=True)
        m_new = jnp.maximum(b_last + m_prev, m_loc)
        a = jnp.exp(b_last + m_prev - m_new)
        e_col = jnp.exp(b_last - b_col + li_col - m_new)
        ke = k.astype(F32) * e_col
        c_ref[hd] = a * c_prev + _dot_tn(ke.astype(BF16), v)
        n_ref[hd] = a * n_prev + jnp.sum(ke, axis=0, keepdims=True)
        m_ref[hd] = jnp.broadcast_to(m_new, m_ref.shape[1:])


def _mlstm(q, k, v, small, o, gn, *, L):
    B, S, _ = q.shape
    blk = pl.BlockSpec((1, L, BW), lambda b, c: (b, c, 0))
    return pl.pallas_call(
        functools.partial(_mlstm_body, L=L),
        grid=(B, S // L),
        in_specs=[blk, blk, blk, pl.BlockSpec((1, L, SMALL_W), lambda b, c: (b, c, 0)), blk,
                  pl.BlockSpec((1, BW), lambda b, c: (0, 0))],
        out_specs=blk,
        out_shape=jax.ShapeDtypeStruct((B, S, BW), BF16),
        scratch_shapes=[pltpu.VMEM((HEADS, ML_DK, HEAD_DV), F32),
                        pltpu.VMEM((HEADS, 1, ML_DK), F32),
                        pltpu.VMEM((HEADS, SUBLANES, LANES), F32)],
        compiler_params=_cparams(("parallel", "arbitrary")),
        name="mlstm",
    )(q, k, v, small, o, gn)


def _gelu_tanh(x):
    return 0.5 * x * (1.0 + jnp.tanh(0.7978845608028654 * (x + 0.044715 * (x * x * x))))


def _lru_body(x_ref, y_ref, cw_ref, cb_ref, wa_ref, ba_ref, wx_ref, bx_ref, lam_ref, o_ref,
              buf_ref, h_ref, *, ts):
    W = x_ref.shape[-1]
    pad = SUBLANES

    @pl.when(pl.program_id(1) == 0)
    def _():
        buf_ref[0:pad, :] = jnp.zeros((pad, W), F32)
        h_ref[...] = jnp.zeros_like(h_ref)

    x = x_ref[0]
    buf_ref[pad:pad + ts, :] = x
    xc = cb_ref[...] + cw_ref[CONV_WIDTH - 1:CONV_WIDTH, :] * x
    for j in range(CONV_WIDTH - 1):
        sh = CONV_WIDTH - 1 - j
        xc = xc + cw_ref[j:j + 1, :] * buf_ref[pad - sh:pad - sh + ts, :]
    buf_ref[0:pad, :] = x[ts - pad:ts, :]

    xb = xc.astype(BF16)
    r = _sigmoid(_dot(xb, wa_ref[...]) + ba_ref[...])
    ig = _sigmoid(_dot(xb, wx_ref[...]) + bx_ref[...])
    log_a = (-LRU_C) * r * _softplus(-lam_ref[...])
    a = jnp.exp(log_a)
    u = jnp.sqrt(-jnp.tanh(log_a) * (a * a + 1.0)) * (ig * xc)

    rid = lax.broadcasted_iota(jnp.int32, (ts, W), 0) & (SUBLANES - 1)
    sh = 1
    while sh < SUBLANES:
        ok = rid >= sh
        a_s = jnp.where(ok, pltpu.roll(a, sh, axis=0), 1.0)
        u_s = jnp.where(ok, pltpu.roll(u, sh, axis=0), 0.0)
        u = u + a * u_s
        a = a * a_s
        sh *= 2

    carry = h_ref[...]
    rows = []
    for gidx in range(ts // SUBLANES):
        sl = slice(gidx * SUBLANES, (gidx + 1) * SUBLANES)
        hg = u[sl, :] + a[sl, :] * carry
        rows.append(hg)
        carry = hg[SUBLANES - 1:SUBLANES, :]
    h_ref[...] = carry
    hs = jnp.concatenate(rows, axis=0)
    o_ref[0] = (hs * _gelu_tanh(y_ref[0])).astype(o_ref.dtype)


def _lru(xb, yb, cw, cb, wa, ba, wx, bx, lam, *, ts):
    B, S, W = xb.shape
    blk = pl.BlockSpec((1, ts, W), lambda b, c: (b, c, 0))
    vec = pl.BlockSpec((1, W), lambda b, c: (0, 0))
    mat = pl.BlockSpec((W, W), lambda b, c: (0, 0))
    return pl.pallas_call(
        functools.partial(_lru_body, ts=ts),
        grid=(B, S // ts),
        in_specs=[blk, blk, pl.BlockSpec((CONV_WIDTH, W), lambda b, c: (0, 0)), vec, mat, vec, mat, vec, vec],
        out_specs=blk,
        out_shape=jax.ShapeDtypeStruct((B, S, W), BF16),
        scratch_shapes=[pltpu.VMEM((ts + SUBLANES, W), F32), pltpu.VMEM((1, W), F32)],
        compiler_params=_cparams(("parallel", "arbitrary")),
        name="rglru",
    )(xb, yb, cw, cb, wa, ba, wx, bx, lam)


def _gla_body(qk_ref, v_ref, g_ref, sm_ref, wal_ref, bal_ref, gn_ref, y_ref, s_ref, *, ts):
    C = GLA_CHUNK
    QW = HEADS * GLA_DK

    @pl.when(pl.program_id(1) == 0)
    def _():
        s_ref[...] = jnp.zeros_like(s_ref)

    ri = lax.broadcasted_iota(jnp.int32, (C, C), 0)
    ci = lax.broadcasted_iota(jnp.int32, (C, C), 1)
    causal = ci <= ri
    tril = causal.astype(F32)
    ones = jnp.ones((C, LANES), F32)

    def chunk(ck, carry):
        r0 = pl.multiple_of(ck * C, C)
        rows = pl.ds(r0, C)
        la = _log_sigmoid(_dot(sm_ref[0, rows, :], wal_ref[...], precision=HIGHEST) + bal_ref[...]) / GLA_TAU
        bc = _dot(tril, la, precision=HIGHEST)
        b_last = bc[C - 1:C, :]
        dec_cols = jnp.exp(_dot_tn(la, ones, precision=HIGHEST))
        qk = qk_ref[0, rows, :]
        e_pos = jnp.exp(bc)
        q_t = (qk[:, 0:QW] * e_pos).astype(BF16)
        k_t = (qk[:, QW:2 * QW] * jnp.exp(-bc)).astype(BF16)
        k_e = (qk[:, QW:2 * QW] * jnp.exp(b_last - bc)).astype(BF16)
        for hd in range(HEADS):
            ks = slice(hd * GLA_DK, (hd + 1) * GLA_DK)
            vs = slice(hd * HEAD_DV, (hd + 1) * HEAD_DV)
            v = v_ref[0, rows, vs]
            s_prev = s_ref[hd]
            attn = jnp.where(causal, _dot_nt(q_t[:, ks], k_t[:, ks]), 0.0)
            o = _dot(attn.astype(BF16), v) + _dot(q_t[:, ks], s_prev.astype(BF16))
            s_ref[hd] = dec_cols[ks, :] * s_prev + _dot_tn(k_e[:, ks], v)
            gg = g_ref[0, rows, vs]
            y = _rms(o, gn_ref[:, vs]) * (gg * _sigmoid(gg))
            y_ref[0, rows, vs] = y.astype(y_ref.dtype)
        return carry

    lax.fori_loop(0, ts // C, chunk, 0)


def _gla(qk, v, g, small, wal, bal, gn, *, ts):
    B, S, _ = v.shape
    QW = HEADS * GLA_DK
    blk = pl.BlockSpec((1, ts, BW), lambda b, c: (b, c, 0))
    const = lambda b, c: (0, 0)
    return pl.pallas_call(
        functools.partial(_gla_body, ts=ts),
        grid=(B, S // ts),
        in_specs=[blk, blk, blk, pl.BlockSpec((1, ts, SMALL_W), lambda b, c: (b, c, 0)),
                  pl.BlockSpec((SMALL_W, QW), const), pl.BlockSpec((1, QW), const),
                  pl.BlockSpec((1, BW), const)],
        out_specs=blk,
        out_shape=jax.ShapeDtypeStruct((B, S, BW), BF16),
        scratch_shapes=[pltpu.VMEM((HEADS, GLA_DK, HEAD_DV), F32)],
        compiler_params=_cparams(("parallel", "arbitrary")),
        name="gla",
    )(qk, v, g, small, wal, bal, gn)


def _merge_body(x_ref, g1_ref, y0_ref, y1_ref, y2_ref, y3_ref, wg_ref, bg_ref, wb_ref, wo_ref, o_ref):
    x = x_ref[...]
    D = x.shape[-1]
    h = _rms(x, g1_ref[...]).astype(BF16)
    merged = None
    for n, y_ref in enumerate((y0_ref, y1_ref, y2_ref, y3_ref)):
        gate = _sigmoid(_dot(h, wg_ref[:, n * D:(n + 1) * D]) + bg_ref[:, n * D:(n + 1) * D])
        term = gate * _dot(y_ref[...], wb_ref[n])
        merged = term if merged is None else merged + term
    o_ref[...] = x + _dot(merged.astype(BF16), wo_ref[...])


def _merge(x2, g1, ys, wg, bg, wb, wo, *, tm):
    T, D = x2.shape
    row = lambda i: (i, 0)
    const = lambda i: (0, 0)
    yspec = pl.BlockSpec((tm, BW), row)
    return pl.pallas_call(
        _merge_body,
        grid=(T // tm,),
        in_specs=[pl.BlockSpec((tm, D), row), pl.BlockSpec((1, D), const), yspec, yspec, yspec, yspec,
                  pl.BlockSpec((D, N_BRANCH * D), const), pl.BlockSpec((1, N_BRANCH * D), const),
                  pl.BlockSpec((N_BRANCH, BW, D), lambda i: (0, 0, 0)), pl.BlockSpec((D, D), const)],
        out_specs=pl.BlockSpec((tm, D), row),
        out_shape=jax.ShapeDtypeStruct((T, D), F32),
        compiler_params=_cparams(("parallel",)),
        name="merge",
    )(x2, g1, *ys, wg, bg, wb, wo)


def _mlp_body(x_ref, g2_ref, wu_ref, wd_ref, o_ref, *, nf):
    x = x_ref[...]
    h = _rms(x, g2_ref[...]).astype(BF16)
    dff = wu_ref.shape[1]
    fc = dff // nf
    acc = x
    for c in range(nf):
        up = jnp.maximum(_dot(h, wu_ref[:, c * fc:(c + 1) * fc]), 0.0)
        acc = acc + _dot((up * up).astype(BF16), wd_ref[c * fc:(c + 1) * fc, :])
    o_ref[...] = acc


def _mlp(x2, g2, wu, wd, *, tm, nf=4):
    T, D = x2.shape
    dff = wu.shape[1]
    row = lambda i: (i, 0)
    const = lambda i: (0, 0)
    return pl.pallas_call(
        functools.partial(_mlp_body, nf=nf),
        grid=(T // tm,),
        in_specs=[pl.BlockSpec((tm, D), row), pl.BlockSpec((1, D), const),
                  pl.BlockSpec((D, dff), const), pl.BlockSpec((dff, D), const)],
        out_specs=pl.BlockSpec((tm, D), row),
        out_shape=jax.ShapeDtypeStruct((T, D), F32),
        compiler_params=_cparams(("parallel",)),
        name="mlp",
    )(x2, g2, wu, wd)


def _in_offsets(D):
    widths = (BW, BW, BW, BW, HEADS, HEADS, BW, BW, BW, HEADS, BW, BW,
              HEADS * GLA_DK, HEADS * GLA_DK, BW, BW, GLA_RANK, N_BRANCH * D)
    offs, acc = [], 0
    for w in widths:
        offs.append(acc)
        acc += w
    return offs, widths


def _prep_inproj(w_in, b_in, D):
    offs, widths = _in_offsets(D)
    (mq, mk, mv, mo, mi, mf, fq, fk, fv, ff, rx, ry, gq, gk, gv, gg, ga, gates) = range(18)
    order = [mq, mk, mv, mo, fq, fk, fv, rx, ry, gq, gk, gv, gg, mi, mf, ff, ga]
    n_small = 3 * HEADS + GLA_RANK

    def gather(a):
        cols = [a[..., offs[i]:offs[i] + widths[i]] for i in order]
        cols.append(jnp.zeros(a.shape[:-1] + (SMALL_W - n_small,), a.dtype))
        return jnp.concatenate(cols, axis=-1)

    w1 = gather(w_in).astype(BF16)
    b1 = gather(b_in[None, :])
    wg = w_in[:, offs[gates]:].astype(BF16)
    bg = b_in[None, offs[gates]:]
    cs = jnp.ones((1, w1.shape[1]), F32)
    cs = cs.at[:, G_MLQ * BW:(G_MLQ + 1) * BW].set(ML_DK ** -0.5)
    cs = cs.at[:, G_GQK * BW:G_GQK * BW + HEADS * GLA_DK].set(GLA_DK ** -0.5)
    return w1, b1, cs, wg, bg


def _block_diag(w):
    nb, n, _ = w.shape
    eye = jnp.eye(nb, dtype=w.dtype)
    return (eye[:, None, :, None] * w[:, :, None, :]).reshape(nb * n, nb * n)


def _layer(x, p, cfg):
    B, S, D = x.shape
    T = B * S
    x2 = x.reshape(T, D)
    w1, b1, cs, wg, bg = _prep_inproj(p["w_in"], p["b_in"], D)
    g1 = p["norm1_g"][None, :]
    outs = _inproj(x2, g1, w1, b1, p["fx_qnorm_g"][None, :], p["fx_knorm_g"][None, :], cs, tm=cfg["tm1"])
    (ml_q, ml_k, ml_v, ml_o, fx_q, fx_k, fx_v, r_x, r_y, g_qk, g_v, g_g, small) = [
        o.reshape(B, S, o.shape[-1]) for o in outs]

    ff = jnp.transpose(small[:, :, SM_FF:SM_FF + HEADS], (0, 2, 1))
    fcum = _fcum(ff.reshape(B * HEADS, S // LANES, LANES)).reshape(B, HEADS, 1, S)
    y_fx = _fox(fx_q, fx_k, fx_v, fcum, tq=cfg["tq"])

    y_ml = _mlstm(ml_q, ml_k, ml_v, small, ml_o, p["ml_norm_g"][None, :], L=cfg["ml_chunk"])

    y_lru = _lru(r_x, r_y, p["lru_conv_w"], p["lru_conv_b"][None, :],
                 _block_diag(p["lru_wa"]).astype(BF16), p["lru_ba"][None, :],
                 _block_diag(p["lru_wx"]).astype(BF16), p["lru_bx"][None, :],
                 p["lru_lambda"][None, :], ts=cfg["ts_lru"])

    wal = jnp.zeros((SMALL_W, HEADS * GLA_DK), F32).at[SM_GA:SM_GA + GLA_RANK, :].set(p["gla_w_alpha"])
    y_gla = _gla(g_qk, g_v, g_g, small, wal, p["gla_b_alpha"][None, :],
                 p["gla_norm_g"][None, :], ts=cfg["ts_gla"])

    ys = [y.reshape(T, BW) for y in (y_ml, y_fx, y_lru, y_gla)]
    x2 = _merge(x2, g1, ys, wg, bg, p["w_branch"].astype(BF16), p["w_o"].astype(BF16), tm=cfg["tm3"])
    x2 = _mlp(x2, p["norm2_g"][None, :], p["w_up"].astype(BF16), p["w_down"].astype(BF16), tm=cfg["tm4"])
    return x2.reshape(B, S, D)


def _config(S):
    return dict(tm1=256, tq=min(512, S), ml_chunk=min(256, S), ts_lru=min(512, S), ts_gla=min(512, S),
                tm3=256, tm4=256)


def kernel(x, norm1_g, w_in, b_in, ml_norm_g, fx_qnorm_g, fx_knorm_g, lru_conv_w, lru_conv_b, lru_wa, lru_ba,
           lru_wx, lru_bx, lru_lambda, gla_w_alpha, gla_b_alpha, gla_norm_g, w_branch, w_o, norm2_g, w_up,
           w_down):
    stacked = dict(norm1_g=norm1_g, w_in=w_in, b_in=b_in, ml_norm_g=ml_norm_g, fx_qnorm_g=fx_qnorm_g,
                   fx_knorm_g=fx_knorm_g, lru_conv_w=lru_conv_w, lru_conv_b=lru_conv_b, lru_wa=lru_wa,
                   lru_ba=lru_ba, lru_wx=lru_wx, lru_bx=lru_bx, lru_lambda=lru_lambda,
                   gla_w_alpha=gla_w_alpha, gla_b_alpha=gla_b_alpha, gla_norm_g=gla_norm_g,
                   w_branch=w_branch, w_o=w_o, norm2_g=norm2_g, w_up=w_up, w_down=w_down)
    cfg = _config(x.shape[1])
    for layer in range(norm1_g.shape[0]):
        x = _layer(x, {name: val[layer] for name, val in stacked.items()}, cfg)
    return x
```

```python
import functools
import math

import jax
import jax.numpy as jnp
from jax import lax
from jax.experimental import pallas as pl
from jax.experimental.pallas import tpu as pltpu

F32 = jnp.float32
BF16 = jnp.bfloat16
HIGHEST = lax.Precision.HIGHEST

N_BRANCH = 4
BRANCH_WIDTH = 512
BW = BRANCH_WIDTH
HEADS = 4
ML_DK = 128
FX_DH = 128
HEAD_DV = 128
CONV_WIDTH = 4
LRU_C = 8.0
GLA_DK = 64
GLA_RANK = 16
GLA_TAU = 16.0
GLA_CHUNK = 64
EPS = 1e-6
LOG2E = math.log2(math.e)

LANES = 128
SUBLANES = 8
BF16_ROWS = 16
VMEM_LIMIT = 56 * 1024 * 1024

SM_MI, SM_MF, SM_FF, SM_GA = 0, 4, 8, 12
SMALL_W = LANES
SMALL_T_ROWS = 16
NEG = -1e30


def _cparams(sem):
    return pltpu.CompilerParams(dimension_semantics=sem, vmem_limit_bytes=VMEM_LIMIT)


def _resident(shape, index_map):
    return pl.BlockSpec(shape, index_map, pipeline_mode=pl.Buffered(1))


def _log_sigmoid(x):
    return jnp.minimum(x, 0.0) - jnp.log1p(jnp.exp(-jnp.abs(x)))


def _softplus(x):
    return jnp.maximum(x, 0.0) + jnp.log1p(jnp.exp(-jnp.abs(x)))


def _sigmoid(x):
    return 1.0 / (1.0 + jnp.exp(-x))


def _rms(x, g):
    return x * lax.rsqrt(jnp.mean(x * x, axis=-1, keepdims=True) + EPS) * g


def _dot(a, b, **kw):
    return jnp.dot(a, b, preferred_element_type=F32, **kw)


def _dot_nt(a, b, **kw):
    return lax.dot_general(a, b, (((1,), (1,)), ((), ())), preferred_element_type=F32, **kw)


def _dot_tn(a, b, **kw):
    return lax.dot_general(a, b, (((0,), (0,)), ((), ())), preferred_element_type=F32, **kw)


def _split3(x):
    hi = x.astype(BF16)
    r1 = x - hi.astype(F32)
    mid = r1.astype(BF16)
    lo = (r1 - mid.astype(F32)).astype(BF16)
    return hi, mid, lo


def _in_offsets(D):
    widths = (BW, BW, BW, BW, HEADS, HEADS, BW, BW, BW, HEADS, BW, BW,
              HEADS * GLA_DK, HEADS * GLA_DK, BW, BW, GLA_RANK, N_BRANCH * D)
    offs, acc = [], 0
    for w in widths:
        offs.append(acc)
        acc += w
    return offs, widths


(C_MQ, C_MK, C_MV, C_MO, C_MI, C_MF, C_FQ, C_FK, C_FV, C_FF, C_RX, C_RY,
 C_GQ, C_GK, C_GV, C_GG, C_GA, C_GATES) = range(18)
G_MLQ, G_MLK, G_MLV, G_MLO, G_FXQ, G_FXK, G_RX, G_RY, G_GQK, G_GV, G_GG = range(11)
N_GROUPS = 11
K1_OUT_DTYPES = (BF16, BF16, BF16, F32, BF16, BF16, F32, F32, F32, BF16, F32)
W1_COLS = (C_MQ, C_MK, C_MV, C_MO, C_FQ, C_FK, C_RX, C_RY, C_GQ, C_GK, C_GV, C_GG)
N1 = N_GROUPS * BW + SMALL_W


def _wprep_body(w_ref, w1_ref, wvt_ref, wst_ref, wg_ref, *, D):
    offs, widths = _in_offsets(D)
    n_in = offs[-1] + widths[-1]

    def cols(c):
        a0 = (offs[c] // LANES) * LANES
        sh = offs[c] - a0
        end = min(a0 + widths[c] + (LANES if sh else 0), n_in)
        return w_ref[0, :, a0:end][:, sh:sh + widths[c]]

    dst = 0
    for c in W1_COLS:
        w1_ref[0, :, dst:dst + widths[c]] = cols(c).astype(BF16)
        dst += widths[c]

    def window(c):
        a0 = (offs[c] // LANES) * LANES
        return w_ref[0, :, a0:a0 + LANES]

    lane = lax.broadcasted_iota(jnp.int32, (w_ref.shape[1], LANES), 1)
    small = jnp.where(lane < SM_FF, window(C_MI),
                      jnp.where(lane < SM_GA, window(C_FF),
                                jnp.where(lane < SM_GA + GLA_RANK, window(C_GA), 0.0)))
    w1_ref[0, :, dst:dst + SMALL_W] = small.astype(BF16)
    wst_ref[0] = small.T[0:SMALL_T_ROWS, :].astype(BF16)
    wvt_ref[0] = cols(C_FV).T.astype(BF16)
    wg_ref[0] = cols(C_GATES).astype(BF16)


def _wprep(w_in, *, rb=128):
    depth, D, n_in = w_in.shape
    offs, _ = _in_offsets(D)
    assert offs[C_MF] == offs[C_MI] + HEADS and offs[C_MI] % LANES == SM_MI
    assert offs[C_FF] % LANES == SM_FF and offs[C_GA] % LANES == SM_GA
    return pl.pallas_call(
        functools.partial(_wprep_body, D=D),
        grid=(depth, D // rb),
        in_specs=[pl.BlockSpec((1, rb, n_in), lambda l, i: (l, i, 0))],
        out_specs=[pl.BlockSpec((1, rb, N1), lambda l, i: (l, i, 0)),
                   pl.BlockSpec((1, BW, rb), lambda l, i: (l, 0, i)),
                   pl.BlockSpec((1, SMALL_T_ROWS, rb), lambda l, i: (l, 0, i)),
                   pl.BlockSpec((1, rb, N_BRANCH * D), lambda l, i: (l, i, 0))],
        out_shape=[jax.ShapeDtypeStruct((depth, D, N1), BF16),
                   jax.ShapeDtypeStruct((depth, BW, D), BF16),
                   jax.ShapeDtypeStruct((depth, SMALL_T_ROWS, D), BF16),
                   jax.ShapeDtypeStruct((depth, D, N_BRANCH * D), BF16)],
        compiler_params=_cparams(("parallel", "parallel")),
        name="wprep",
    )(w_in)


def _inproj_body(x_ref, g1_ref, w_ref, b_ref, gq_ref, gk_ref, cs_ref, wvt_ref, bvt_ref, wst_ref, bst_ref,
                 *out_refs):
    x = x_ref[...]
    h = _rms(x, g1_ref[...]).astype(BF16)

    def proj(off, width):
        return _dot(h, w_ref[0, :, off:off + width]) + b_ref[:, off:off + width]

    for gi in range(N_GROUPS):
        z = proj(gi * BW, BW)
        o_ref = out_refs[gi]
        if gi in (G_FXQ, G_FXK):
            g = gq_ref[...] if gi == G_FXQ else gk_ref[...]
            scale = FX_DH ** -0.5 * LOG2E if gi == G_FXQ else 1.0
            for hd in range(HEADS):
                zh = z[:, hd * FX_DH:(hd + 1) * FX_DH]
                o_ref[:, hd * FX_DH:(hd + 1) * FX_DH] = (_rms(zh, g) * scale).astype(o_ref.dtype)
        elif gi in (G_MLQ, G_GQK):
            o_ref[...] = (z * cs_ref[:, gi * BW:(gi + 1) * BW]).astype(o_ref.dtype)
        else:
            o_ref[...] = z.astype(o_ref.dtype)
    out_refs[N_GROUPS][...] = proj(N_GROUPS * BW, SMALL_W)
    out_refs[N_GROUPS + 1][...] = (_dot_nt(wvt_ref[0], h) + bvt_ref[...]).astype(BF16)
    out_refs[N_GROUPS + 2][...] = _dot_nt(wst_ref[0], h) + bst_ref[...]


def _inproj(x2, g1, w1, b1, gq, gk, cs, wvt, bvt, wst, bst, *, layer, tm):
    T, D = x2.shape
    const = lambda i: (0, 0)
    lay = lambda i: (layer, 0, 0)
    row = lambda i: (i, 0)
    col = lambda i: (0, i)
    out_shape = [jax.ShapeDtypeStruct((T, BW), dt) for dt in K1_OUT_DTYPES]
    out_shape += [jax.ShapeDtypeStruct((T, SMALL_W), F32), jax.ShapeDtypeStruct((BW, T), BF16),
                  jax.ShapeDtypeStruct((SMALL_T_ROWS, T), F32)]
    out_specs = [pl.BlockSpec((tm, BW), row) for _ in K1_OUT_DTYPES]
    out_specs += [pl.BlockSpec((tm, SMALL_W), row), pl.BlockSpec((BW, tm), col),
                  pl.BlockSpec((SMALL_T_ROWS, tm), col)]
    return pl.pallas_call(
        _inproj_body,
        grid=(T // tm,),
        in_specs=[
            pl.BlockSpec((tm, D), row),
            _resident((1, D), const),
            _resident((1, D, N1), lay),
            _resident((1, N1), const),
            _resident((1, FX_DH), const),
            _resident((1, FX_DH), const),
            _resident((1, N1), const),
            _resident((1, BW, D), lay),
            _resident((BW, 1), const),
            _resident((1, SMALL_T_ROWS, D), lay),
            _resident((SMALL_T_ROWS, 1), const),
        ],
        out_specs=out_specs,
        out_shape=out_shape,
        compiler_params=_cparams(("parallel",)),
        name="inproj",
    )(x2, g1, w1, b1, gq, gk, cs, wvt, bvt, wst, bst)


def _fcum_body(f_ref, o_ref):
    ls = _log_sigmoid(f_ref[0, 0]) * LOG2E
    nb = ls.shape[0]
    ii = lax.broadcasted_iota(jnp.int32, (LANES, LANES), 0)
    jj = lax.broadcasted_iota(jnp.int32, (LANES, LANES), 1)
    local = _dot(ls, (ii <= jj).astype(F32), precision=HIGHEST)
    tot = jnp.broadcast_to(local[:, LANES - 1:LANES], (nb, LANES))
    ri = lax.broadcasted_iota(jnp.int32, (nb, nb), 0)
    ci = lax.broadcasted_iota(jnp.int32, (nb, nb), 1)
    offs = _dot((ci < ri).astype(F32), tot, precision=HIGHEST)
    o_ref[0, 0] = local + offs


def _fcum(small_t4):
    _, B, nb, _ = small_t4.shape
    return pl.pallas_call(
        _fcum_body, grid=(B, HEADS),
        in_specs=[pl.BlockSpec((1, 1, nb, LANES), lambda b, h: (SM_FF + h, b, 0, 0))],
        out_specs=pl.BlockSpec((1, 1, nb, LANES), lambda b, h: (b, h, 0, 0)),
        out_shape=jax.ShapeDtypeStruct((B, HEADS, nb, LANES), F32),
        compiler_params=_cparams(("parallel", "parallel")), name="fox_cumsum",
    )(small_t4)


def _fox_body(q_ref, k_ref, vt_ref, f_ref, o_ref, e_ref, st_ref, *, tq, tk):
    qi = pl.program_id(2)
    S = e_ref.shape[0]
    nb = tq // tk

    @pl.when(qi == 0)
    def _():
        rid = lax.broadcasted_iota(jnp.int32, (LANES, LANES), 0)

        def build(blk, carry):
            s0 = pl.multiple_of(blk * LANES, LANES)
            hi, mid, lo = _split3(-f_ref[0, 0, :, pl.ds(s0, LANES)])
            bc = lambda part: jnp.broadcast_to(part.astype(F32), (LANES, LANES))
            parts = jnp.where(rid == 0, bc(hi), jnp.where(rid == 1, bc(mid), jnp.where(rid == 2, bc(lo), 0.0)))
            e_ref[pl.ds(s0, LANES), :] = parts.T.astype(BF16)
            return carry

        lax.fori_loop(0, S // LANES, build, 0)

    lane = lax.broadcasted_iota(jnp.int32, (tq, LANES), 1)
    q_aug = jnp.concatenate([q_ref[0], jnp.where(lane < 3, 1.0, 0.0).astype(BF16)], axis=1)

    ones_rows = jnp.ones((BF16_ROWS, tk), BF16)

    def scores(kj, slot, q_lo=0):
        k0 = pl.multiple_of(kj * tk, tk)
        k_aug = jnp.concatenate([k_ref[0, pl.ds(k0, tk), :], e_ref[pl.ds(k0, tk), :]], axis=1)
        st_ref[slot, :, q_lo:] = _dot_nt(k_aug, q_aug[q_lo:, :])

    def fold(kj, slot, m, acc, q_lo=0, diagonal=False):
        st = st_ref[slot, :, q_lo:]
        if diagonal:
            key = lax.broadcasted_iota(jnp.int32, st.shape, 0)
            qry = lax.broadcasted_iota(jnp.int32, st.shape, 1)
            st = jnp.where(key <= qry, st, NEG)
        m_new = jnp.maximum(m[:, q_lo:], jnp.max(st, axis=0, keepdims=True))
        alpha = jnp.exp2(m[:, q_lo:] - m_new)
        p = jnp.exp2(st - m_new).astype(BF16)
        k0 = pl.multiple_of(kj * tk, tk)
        v_aug = jnp.concatenate([vt_ref[:, pl.ds(k0, tk)], ones_rows], axis=0)
        acc_new = alpha * acc[:, q_lo:] + _dot(v_aug, p)
        if q_lo:
            m_new = jnp.concatenate([m[:, :q_lo], m_new], axis=1)
            acc_new = jnp.concatenate([acc[:, :q_lo], acc_new], axis=1)
        return m_new, acc_new

    def body(i, carry):
        m, acc = carry
        for j in range(nb):
            scores(nb * i + j + 1, (j + 1) % nb)
            m, acc = fold(nb * i + j, j, m, acc)
        return m, acc

    scores(0, 0)
    init = (jnp.full((1, tq), NEG, F32), jnp.zeros((HEAD_DV + BF16_ROWS, tq), F32))
    m, acc = lax.fori_loop(0, qi, body, init)
    for j in range(nb):
        if j + 1 < nb:
            scores(nb * qi + j + 1, j + 1, q_lo=(j + 1) * tk)
        m, acc = fold(nb * qi + j, j, m, acc, q_lo=j * tk, diagonal=True)
    out_t = acc[0:HEAD_DV, :] / acc[HEAD_DV:HEAD_DV + 1, :]
    o_ref[0] = out_t.T.astype(o_ref.dtype)


def _fox(q, k, vt, fcum, *, tq, tk):
    B, S, _ = q.shape
    qspec = pl.BlockSpec((1, tq, FX_DH), lambda b, h, i: (b, i, h))
    return pl.pallas_call(
        functools.partial(_fox_body, tq=tq, tk=tk),
        grid=(B, HEADS, S // tq),
        in_specs=[qspec,
                  pl.BlockSpec((1, S, FX_DH), lambda b, h, i: (b, 0, h)),
                  pl.BlockSpec((HEAD_DV, S), lambda b, h, i: (h, b)),
                  pl.BlockSpec((1, 1, 1, S), lambda b, h, i: (b, h, 0, 0))],
        out_specs=qspec,
        out_shape=jax.ShapeDtypeStruct((B, S, BW), BF16),
        scratch_shapes=[pltpu.VMEM((S, LANES), BF16), pltpu.VMEM((tq // tk, tk, tq), F32)],
        compiler_params=_cparams(("parallel", "parallel", "arbitrary")),
        name="fox_attn",
    )(q, k, vt, fcum)


def _mlstm_body(q_ref, k_ref, v_ref, sm_ref, smt_ref, o_ref, gn_ref, y_ref, c_ref, n_ref, m_ref, *, L):
    @pl.when(pl.program_id(1) == 0)
    def _():
        c_ref[...] = jnp.zeros_like(c_ref)
        n_ref[...] = jnp.zeros_like(n_ref)
        m_ref[...] = jnp.zeros_like(m_ref)

    g = sm_ref[0]
    gt = smt_ref[...]
    lf = _log_sigmoid(g)
    lft = _log_sigmoid(gt[0:SUBLANES, :])
    ri = lax.broadcasted_iota(jnp.int32, (L, L), 0)
    ci = lax.broadcasted_iota(jnp.int32, (L, L), 1)
    causal = ci <= ri
    b_cols = _dot(causal.astype(F32), lf, precision=HIGHEST)
    b_rows = _dot(lft, (ri <= ci).astype(F32), precision=HIGHEST)

    for hd in range(HEADS):
        sl = slice(hd * HEAD_DV, (hd + 1) * HEAD_DV)
        q = q_ref[0, :, sl]
        k = k_ref[0, :, sl]
        v = v_ref[0, :, sl]
        b_col = b_cols[:, SM_MF + hd:SM_MF + hd + 1]
        b_row = b_rows[SM_MF + hd:SM_MF + hd + 1, :]
        li_col = g[:, SM_MI + hd:SM_MI + hd + 1]
        li_row = gt[SM_MI + hd:SM_MI + hd + 1, :]
        b_last = b_col[L - 1:L, :]
        c_prev = c_ref[hd]
        n_prev = n_ref[hd]
        m_prev = m_ref[hd][0:1, 0:1]

        d = jnp.where(causal, b_col - b_row + li_row, -jnp.inf)
        inter = b_col + m_prev
        m_t = jnp.maximum(inter, jnp.max(d, axis=-1, keepdims=True))
        s_qk = _dot_nt(q, k) * jnp.exp(d - m_t)
        w_inter = jnp.exp(inter - m_t)
        num = _dot(s_qk.astype(BF16), v) + w_inter * _dot(q, c_prev.astype(BF16))
        qn = jnp.sum(q.astype(F32) * n_prev, axis=-1, keepdims=True)
        den = jnp.sum(s_qk, axis=-1, keepdims=True) + w_inter * qn
        hh = num / jnp.maximum(jnp.abs(den), jnp.exp(-m_t))
        y = _rms(hh, gn_ref[:, sl]) * _sigmoid(o_ref[0, :, sl])
        y_ref[0, :, sl] = y.astype(y_ref.dtype)

        m_loc = jnp.max(b_last - b_row + li_row, axis=-1, keepdims=True)
        m_new = jnp.maximum(b_last + m_prev, m_loc)
        a = jnp.exp(b_last + m_prev - m_new)
        e_col = jnp.exp(b_last - b_col + li_col - m_new)
        ke = k.astype(F32) * e_col
        c_ref[hd] = a * c_prev + _dot_tn(ke.astype(BF16), v)
        n_ref[hd] = a * n_prev + jnp.sum(ke, axis=0, keepdims=True)
        m_ref[hd] = jnp.broadcast_to(m_new, m_ref.shape[1:])


def _mlstm(q, k, v, small, small_t, o, gn, *, L):
    B, S, _ = q.shape
    nc = S // L
    blk = pl.BlockSpec((1, L, BW), lambda b, c: (b, c, 0))
    return pl.pallas_call(
        functools.partial(_mlstm_body, L=L),
        grid=(B, nc),
        in_specs=[blk, blk, blk, pl.BlockSpec((1, L, SMALL_W), lambda b, c: (b, c, 0)),
                  pl.BlockSpec((SMALL_T_ROWS, L), lambda b, c: (0, b * nc + c)), blk,
                  pl.BlockSpec((1, BW), lambda b, c: (0, 0))],
        out_specs=blk,
        out_shape=jax.ShapeDtypeStruct((B, S, BW), BF16),
        scratch_shapes=[pltpu.VMEM((HEADS, ML_DK, HEAD_DV), F32),
                        pltpu.VMEM((HEADS, 1, ML_DK), F32),
                        pltpu.VMEM((HEADS, SUBLANES, LANES), F32)],
        compiler_params=_cparams(("parallel", "arbitrary")),
        name="mlstm",
    )(q, k, v, small, small_t, o, gn)


def _gelu_tanh(x):
    return 0.5 * x * (1.0 + jnp.tanh(0.7978845608028654 * (x + 0.044715 * (x * x * x))))


def _lru_body(x_ref, y_ref, cw_ref, cb_ref, wa_ref, ba_ref, wx_ref, bx_ref, lam_ref, o_ref,
              buf_ref, h_ref, *, ts):
    W = x_ref.shape[-1]
    pad = SUBLANES

    @pl.when(pl.program_id(1) == 0)
    def _():
        buf_ref[0:pad, :] = jnp.zeros((pad, W), F32)
        h_ref[...] = jnp.zeros_like(h_ref)

    x = x_ref[0]
    buf_ref[pad:pad + ts, :] = x
    xc = cb_ref[...] + cw_ref[CONV_WIDTH - 1:CONV_WIDTH, :] * x
    for j in range(CONV_WIDTH - 1):
        sh = CONV_WIDTH - 1 - j
        xc = xc + cw_ref[j:j + 1, :] * buf_ref[pad - sh:pad - sh + ts, :]
    buf_ref[0:pad, :] = x[ts - pad:ts, :]

    xb = xc.astype(BF16)
    r = _sigmoid(_dot(xb, wa_ref[...]) + ba_ref[...])
    ig = _sigmoid(_dot(xb, wx_ref[...]) + bx_ref[...])
    log_a = (-LRU_C) * r * _softplus(-lam_ref[...])
    a = jnp.exp(log_a)
    u = jnp.sqrt(-jnp.tanh(log_a) * (a * a + 1.0)) * (ig * xc)

    rid = lax.broadcasted_iota(jnp.int32, (ts, W), 0) & (SUBLANES - 1)
    sh = 1
    while sh < SUBLANES:
        ok = rid >= sh
        a_s = jnp.where(ok, pltpu.roll(a, sh, axis=0), 1.0)
        u_s = jnp.where(ok, pltpu.roll(u, sh, axis=0), 0.0)
        u = u + a * u_s
        a = a * a_s
        sh *= 2

    carry = h_ref[...]
    rows = []
    for gidx in range(ts // SUBLANES):
        sl = slice(gidx * SUBLANES, (gidx + 1) * SUBLANES)
        hg = u[sl, :] + a[sl, :] * carry
        rows.append(hg)
        carry = hg[SUBLANES - 1:SUBLANES, :]
    h_ref[...] = carry
    hs = jnp.concatenate(rows, axis=0)
    o_ref[0] = (hs * _gelu_tanh(y_ref[0])).astype(o_ref.dtype)


def _lru(xb, yb, cw, cb, wa, ba, wx, bx, lam, *, ts):
    B, S, W = xb.shape
    blk = pl.BlockSpec((1, ts, W), lambda b, c: (b, c, 0))
    vec = pl.BlockSpec((1, W), lambda b, c: (0, 0))
    mat = pl.BlockSpec((W, W), lambda b, c: (0, 0))
    return pl.pallas_call(
        functools.partial(_lru_body, ts=ts),
        grid=(B, S // ts),
        in_specs=[blk, blk, pl.BlockSpec((CONV_WIDTH, W), lambda b, c: (0, 0)), vec, mat, vec, mat, vec, vec],
        out_specs=blk,
        out_shape=jax.ShapeDtypeStruct((B, S, W), BF16),
        scratch_shapes=[pltpu.VMEM((ts + SUBLANES, W), F32), pltpu.VMEM((1, W), F32)],
        compiler_params=_cparams(("parallel", "arbitrary")),
        name="rglru",
    )(xb, yb, cw, cb, wa, ba, wx, bx, lam)


def _gla_body(qk_ref, v_ref, g_ref, sm_ref, wal_ref, bal_ref, gn_ref, y_ref, s_ref, *, ts):
    C = GLA_CHUNK
    QW = HEADS * GLA_DK
    nchunk = ts // C

    @pl.when(pl.program_id(1) == 0)
    def _():
        s_ref[...] = jnp.zeros_like(s_ref)

    la = _log_sigmoid(_dot(sm_ref[0], wal_ref[...], precision=HIGHEST) + bal_ref[...]) / GLA_TAU
    ri = lax.broadcasted_iota(jnp.int32, (ts, ts), 0)
    ci = lax.broadcasted_iota(jnp.int32, (ts, ts), 1)
    shift = C.bit_length() - 1
    tril_blocks = jnp.where((ri >> shift) == (ci >> shift), jnp.where(ci <= ri, 1.0, 0.0), 0.0).astype(BF16)
    parts = jnp.concatenate(_split3(la), axis=1)
    r3 = _dot(tril_blocks, parts)
    bc = r3[:, 0:QW] + r3[:, QW:2 * QW] + r3[:, 2 * QW:3 * QW]
    b_last = jnp.concatenate(
        [jnp.broadcast_to(bc[(c + 1) * C - 1:(c + 1) * C, :], (C, QW)) for c in range(nchunk)], axis=0)
    qk = qk_ref[0]
    q_t = (qk[:, 0:QW] * jnp.exp(bc)).astype(BF16)
    k = qk[:, QW:2 * QW]
    k_t = (k * jnp.exp(-bc)).astype(BF16)
    k_e = (k * jnp.exp(b_last - bc)).astype(BF16)
    dec = jnp.exp(b_last)

    lane = lax.broadcasted_iota(jnp.int32, (ts, LANES), 1)
    cr = lax.broadcasted_iota(jnp.int32, (C, C), 0)
    cc = lax.broadcasted_iota(jnp.int32, (C, C), 1)
    causal = cc <= cr
    zero = jnp.zeros((), BF16)
    for hd in range(HEADS):
        pair = slice((hd // 2) * LANES, (hd // 2 + 1) * LANES)
        mine = (lane >= GLA_DK) if hd % 2 else (lane < GLA_DK)
        q_h = q_t[:, pair]
        kt_h = jnp.where(mine, k_t[:, pair], zero)
        ke_h = jnp.where(mine, k_e[:, pair], zero)
        dec_h = dec[:, pair]
        vs = slice(hd * HEAD_DV, (hd + 1) * HEAD_DV)
        st = s_ref[hd]
        for c in range(nchunk):
            rows = slice(c * C, (c + 1) * C)
            v = v_ref[0, rows, vs]
            attn = jnp.where(causal, _dot_nt(q_h[rows], kt_h[rows]), 0.0)
            o = _dot(attn.astype(BF16), v) + _dot_nt(q_h[rows], st.astype(BF16))
            st = st * dec_h[c * C:c * C + 1, :] + _dot_tn(v, ke_h[rows])
            gg = g_ref[0, rows, vs]
            y = _rms(o, gn_ref[:, vs]) * (gg * _sigmoid(gg))
            y_ref[0, rows, vs] = y.astype(y_ref.dtype)
        s_ref[hd] = st


def _gla(qk, v, g, small, wal, bal, gn, *, ts):
    B, S, _ = v.shape
    QW = HEADS * GLA_DK
    blk = pl.BlockSpec((1, ts, BW), lambda b, c: (b, c, 0))
    const = lambda b, c: (0, 0)
    return pl.pallas_call(
        functools.partial(_gla_body, ts=ts),
        grid=(B, S // ts),
        in_specs=[blk, blk, blk, pl.BlockSpec((1, ts, SMALL_W), lambda b, c: (b, c, 0)),
                  pl.BlockSpec((SMALL_W, QW), const), pl.BlockSpec((1, QW), const),
                  pl.BlockSpec((1, BW), const)],
        out_specs=blk,
        out_shape=jax.ShapeDtypeStruct((B, S, BW), BF16),
        scratch_shapes=[pltpu.VMEM((HEADS, HEAD_DV, LANES), F32)],
        compiler_params=_cparams(("parallel", "arbitrary")),
        name="gla",
    )(qk, v, g, small, wal, bal, gn)


def _merge_body(x_ref, g1_ref, y0_ref, y1_ref, y2_ref, y3_ref, wg_ref, bg_ref, wb_ref, wo_ref, o_ref):
    x = x_ref[...]
    D = x.shape[-1]
    h = _rms(x, g1_ref[...]).astype(BF16)
    merged = None
    for n, y_ref in enumerate((y0_ref, y1_ref, y2_ref, y3_ref)):
        gate = _sigmoid(_dot(h, wg_ref[0, :, n * D:(n + 1) * D]) + bg_ref[:, n * D:(n + 1) * D])
        term = gate * _dot(y_ref[...], wb_ref[n])
        merged = term if merged is None else merged + term
    o_ref[...] = x + _dot(merged.astype(BF16), wo_ref[...])


def _merge(x2, g1, ys, wg, bg, wb, wo, *, layer, tm):
    T, D = x2.shape
    row = lambda i: (i, 0)
    const = lambda i: (0, 0)
    yspec = pl.BlockSpec((tm, BW), row)
    return pl.pallas_call(
        _merge_body,
        grid=(T // tm,),
        in_specs=[pl.BlockSpec((tm, D), row), _resident((1, D), const), yspec, yspec, yspec, yspec,
                  _resident((1, D, N_BRANCH * D), lambda i: (layer, 0, 0)), _resident((1, N_BRANCH * D), const),
                  _resident((N_BRANCH, BW, D), lambda i: (0, 0, 0)), _resident((D, D), const)],
        out_specs=pl.BlockSpec((tm, D), row),
        out_shape=jax.ShapeDtypeStruct((T, D), F32),
        compiler_params=_cparams(("parallel",)),
        name="merge",
    )(x2, g1, *ys, wg, bg, wb, wo)


def _mlp_body(x_ref, g2_ref, wu_ref, wd_ref, o_ref, *, nf):
    x = x_ref[...]
    h = _rms(x, g2_ref[...]).astype(BF16)
    dff = wu_ref.shape[1]
    fc = dff // nf
    acc = x
    for c in range(nf):
        up = jnp.maximum(_dot(h, wu_ref[:, c * fc:(c + 1) * fc]), 0.0)
        acc = acc + _dot((up * up).astype(BF16), wd_ref[c * fc:(c + 1) * fc, :])
    o_ref[...] = acc


def _mlp(x2, g2, wu, wd, *, tm, nf=4):
    T, D = x2.shape
    dff = wu.shape[1]
    row = lambda i: (i, 0)
    const = lambda i: (0, 0)
    return pl.pallas_call(
        functools.partial(_mlp_body, nf=nf),
        grid=(T // tm,),
        in_specs=[pl.BlockSpec((tm, D), row), _resident((1, D), const),
                  _resident((D, dff), const), _resident((dff, D), const)],
        out_specs=pl.BlockSpec((tm, D), row),
        out_shape=jax.ShapeDtypeStruct((T, D), F32),
        compiler_params=_cparams(("parallel",)),
        name="mlp",
    )(x2, g2, wu, wd)


def _prep_bias(b_in, D):
    offs, widths = _in_offsets(D)
    take = lambda c: b_in[offs[c]:offs[c] + widths[c]]
    n_small = 3 * HEADS + GLA_RANK
    b_small = jnp.concatenate([take(C_MI), take(C_MF), take(C_FF), take(C_GA), jnp.zeros((SMALL_W - n_small,), F32)])
    b1 = jnp.concatenate([take(c) for c in W1_COLS] + [b_small])[None, :]
    bvt = take(C_FV)[:, None]
    bst = b_small[:SMALL_T_ROWS, None]
    bg = take(C_GATES)[None, :]
    cs = jnp.ones((1, N1), F32)
    cs = cs.at[:, G_MLQ * BW:(G_MLQ + 1) * BW].set(ML_DK ** -0.5)
    cs = cs.at[:, G_GQK * BW:G_GQK * BW + HEADS * GLA_DK].set(GLA_DK ** -0.5)
    return b1, bvt, bst, bg, cs


def _block_diag(w):
    nb, n, _ = w.shape
    eye = jnp.eye(nb, dtype=w.dtype)
    return (eye[:, None, :, None] * w[:, :, None, :]).reshape(nb * n, nb * n)


def _layer(x, p, w1, wvt, wst, wg, layer, cfg):
    B, S, D = x.shape
    T = B * S
    x2 = x.reshape(T, D)
    b1, bvt, bst, bg, cs = _prep_bias(p["b_in"], D)
    g1 = p["norm1_g"][None, :]
    outs = _inproj(x2, g1, w1, b1, p["fx_qnorm_g"][None, :], p["fx_knorm_g"][None, :], cs, wvt, bvt, wst, bst,
                   layer=layer, tm=cfg["tm1"])
    fx_vt, small_t = outs[N_GROUPS + 1], outs[N_GROUPS + 2]
    (ml_q, ml_k, ml_v, ml_o, fx_q, fx_k, r_x, r_y, g_qk, g_v, g_g, small) = [
        o.reshape(B, S, o.shape[-1]) for o in outs[:N_GROUPS + 1]]

    fcum = _fcum(small_t.reshape(SMALL_T_ROWS, B, S // LANES, LANES)).reshape(B, HEADS, 1, S)
    y_fx = _fox(fx_q, fx_k, fx_vt, fcum, tq=cfg["tq"], tk=cfg["tk"])

    y_ml = _mlstm(ml_q, ml_k, ml_v, small, small_t, ml_o, p["ml_norm_g"][None, :], L=cfg["ml_chunk"])

    y_lru = _lru(r_x, r_y, p["lru_conv_w"], p["lru_conv_b"][None, :],
                 _block_diag(p["lru_wa"]).astype(BF16), p["lru_ba"][None, :],
                 _block_diag(p["lru_wx"]).astype(BF16), p["lru_bx"][None, :],
                 p["lru_lambda"][None, :], ts=cfg["ts_lru"])

    wal = jnp.zeros((SMALL_W, HEADS * GLA_DK), F32).at[SM_GA:SM_GA + GLA_RANK, :].set(p["gla_w_alpha"])
    y_gla = _gla(g_qk, g_v, g_g, small, wal, p["gla_b_alpha"][None, :],
                 p["gla_norm_g"][None, :], ts=cfg["ts_gla"])

    ys = [y.reshape(T, BW) for y in (y_ml, y_fx, y_lru, y_gla)]
    x2 = _merge(x2, g1, ys, wg, bg, p["w_branch"].astype(BF16), p["w_o"].astype(BF16), layer=layer, tm=cfg["tm3"])
    x2 = _mlp(x2, p["norm2_g"][None, :], p["w_up"].astype(BF16), p["w_down"].astype(BF16), tm=cfg["tm4"])
    return x2.reshape(B, S, D)


def _config(S):
    return dict(tm1=512, tq=min(1024, S), tk=256, ml_chunk=min(256, S), ts_lru=min(512, S), ts_gla=min(512, S),
                tm3=512, tm4=512)


def kernel(x, norm1_g, w_in, b_in, ml_norm_g, fx_qnorm_g, fx_knorm_g, lru_conv_w, lru_conv_b, lru_wa, lru_ba,
           lru_wx, lru_bx, lru_lambda, gla_w_alpha, gla_b_alpha, gla_norm_g, w_branch, w_o, norm2_g, w_up,
           w_down):
    stacked = dict(norm1_g=norm1_g, b_in=b_in, ml_norm_g=ml_norm_g, fx_qnorm_g=fx_qnorm_g,
                   fx_knorm_g=fx_knorm_g, lru_conv_w=lru_conv_w, lru_conv_b=lru_conv_b, lru_wa=lru_wa,
                   lru_ba=lru_ba, lru_wx=lru_wx, lru_bx=lru_bx, lru_lambda=lru_lambda,
                   gla_w_alpha=gla_w_alpha, gla_b_alpha=gla_b_alpha, gla_norm_g=gla_norm_g,
                   w_branch=w_branch, w_o=w_o, norm2_g=norm2_g, w_up=w_up, w_down=w_down)
    cfg = _config(x.shape[1])
    w1, wvt, wst, wg = _wprep(w_in)
    for layer in range(norm1_g.shape[0]):
        x = _layer(x, {name: val[layer] for name, val in stacked.items()}, w1, wvt, wst, wg, layer, cfg)
    return x
```

```python
import functools
import math

import jax
import jax.numpy as jnp
from jax import lax
from jax.experimental import pallas as pl
from jax.experimental.pallas import tpu as pltpu

F32 = jnp.float32
BF16 = jnp.bfloat16
HIGHEST = lax.Precision.HIGHEST

N_BRANCH = 4
BRANCH_WIDTH = 512
BW = BRANCH_WIDTH
HEADS = 4
ML_DK = 128
FX_DH = 128
HEAD_DV = 128
CONV_WIDTH = 4
LRU_C = 8.0
GLA_DK = 64
GLA_RANK = 16
GLA_TAU = 16.0
GLA_CHUNK = 64
FOX_HEADS_PER_STEP = 2
EPS = 1e-6
LOG2E = math.log2(math.e)

LANES = 128
SUBLANES = 8
BF16_ROWS = 16
VMEM_LIMIT = 56 * 1024 * 1024

SM_MI, SM_MF, SM_FF, SM_GA = 0, 4, 8, 12
SMALL_W = LANES
SMALL_T_ROWS = 16
NEG = -1e30


def _cparams(sem):
    return pltpu.CompilerParams(dimension_semantics=sem, vmem_limit_bytes=VMEM_LIMIT)


def _resident(shape, index_map):
    return pl.BlockSpec(shape, index_map, pipeline_mode=pl.Buffered(1))


def _log_sigmoid(x):
    return jnp.minimum(x, 0.0) - jnp.log1p(jnp.exp(-jnp.abs(x)))


def _softplus(x):
    return jnp.maximum(x, 0.0) + jnp.log1p(jnp.exp(-jnp.abs(x)))


def _sigmoid(x):
    return 0.5 * jnp.tanh(0.5 * x) + 0.5


def _rms(x, g):
    return x * lax.rsqrt(jnp.mean(x * x, axis=-1, keepdims=True) + EPS) * g


def _dot(a, b, **kw):
    return jnp.dot(a, b, preferred_element_type=F32, **kw)


def _dot_nt(a, b, **kw):
    return lax.dot_general(a, b, (((1,), (1,)), ((), ())), preferred_element_type=F32, **kw)


def _dot_tn(a, b, **kw):
    return lax.dot_general(a, b, (((0,), (0,)), ((), ())), preferred_element_type=F32, **kw)


def _split3(x):
    hi = x.astype(BF16)
    r1 = x - hi.astype(F32)
    mid = r1.astype(BF16)
    lo = (r1 - mid.astype(F32)).astype(BF16)
    return hi, mid, lo


def _in_offsets(D):
    widths = (BW, BW, BW, BW, HEADS, HEADS, BW, BW, BW, HEADS, BW, BW,
              HEADS * GLA_DK, HEADS * GLA_DK, BW, BW, GLA_RANK, N_BRANCH * D)
    offs, acc = [], 0
    for w in widths:
        offs.append(acc)
        acc += w
    return offs, widths


(C_MQ, C_MK, C_MV, C_MO, C_MI, C_MF, C_FQ, C_FK, C_FV, C_FF, C_RX, C_RY,
 C_GQ, C_GK, C_GV, C_GG, C_GA, C_GATES) = range(18)
G_MLK, G_MLO, G_FXQ, G_FXK, G_RX, G_RY, G_GQK, G_GV, G_GG = range(9)
N_GROUPS = 9
K1_OUT_DTYPES = (BF16, F32, BF16, BF16, F32, F32, F32, BF16, F32)
W1_COLS = (C_MK, C_MO, C_FQ, C_FK, C_RX, C_RY, C_GQ, C_GK, C_GV, C_GG)
N1 = N_GROUPS * BW + SMALL_W
WT_COLS = (C_MQ, C_MV, C_FV)
NT = len(WT_COLS) * BW


def _wprep_body(w_ref, w1_ref, wvt_ref, wst_ref, wg_ref, *, D):
    offs, widths = _in_offsets(D)
    n_in = offs[-1] + widths[-1]

    def cols(c):
        a0 = (offs[c] // LANES) * LANES
        sh = offs[c] - a0
        end = min(a0 + widths[c] + (LANES if sh else 0), n_in)
        return w_ref[0, :, a0:end][:, sh:sh + widths[c]]

    dst = 0
    for c in W1_COLS:
        w1_ref[0, :, dst:dst + widths[c]] = cols(c).astype(BF16)
        dst += widths[c]

    def window(c):
        a0 = (offs[c] // LANES) * LANES
        return w_ref[0, :, a0:a0 + LANES]

    lane = lax.broadcasted_iota(jnp.int32, (w_ref.shape[1], LANES), 1)
    small = jnp.where(lane < SM_FF, window(C_MI),
                      jnp.where(lane < SM_GA, window(C_FF),
                                jnp.where(lane < SM_GA + GLA_RANK, window(C_GA), 0.0)))
    w1_ref[0, :, dst:dst + SMALL_W] = small.astype(BF16)
    wst_ref[0] = small.T[0:SMALL_T_ROWS, :].astype(BF16)
    for n, c in enumerate(WT_COLS):
        wvt_ref[0, n * BW:(n + 1) * BW, :] = cols(c).T.astype(BF16)
    wg_ref[0] = cols(C_GATES).astype(BF16)


def _wprep(w_in, *, rb=128):
    depth, D, n_in = w_in.shape
    offs, _ = _in_offsets(D)
    assert offs[C_MF] == offs[C_MI] + HEADS and offs[C_MI] % LANES == SM_MI
    assert offs[C_FF] % LANES == SM_FF and offs[C_GA] % LANES == SM_GA
    return pl.pallas_call(
        functools.partial(_wprep_body, D=D),
        grid=(depth, D // rb),
        in_specs=[pl.BlockSpec((1, rb, n_in), lambda l, i: (l, i, 0))],
        out_specs=[pl.BlockSpec((1, rb, N1), lambda l, i: (l, i, 0)),
                   pl.BlockSpec((1, NT, rb), lambda l, i: (l, 0, i)),
                   pl.BlockSpec((1, SMALL_T_ROWS, rb), lambda l, i: (l, 0, i)),
                   pl.BlockSpec((1, rb, N_BRANCH * D), lambda l, i: (l, i, 0))],
        out_shape=[jax.ShapeDtypeStruct((depth, D, N1), BF16),
                   jax.ShapeDtypeStruct((depth, NT, D), BF16),
                   jax.ShapeDtypeStruct((depth, SMALL_T_ROWS, D), BF16),
                   jax.ShapeDtypeStruct((depth, D, N_BRANCH * D), BF16)],
        compiler_params=_cparams(("parallel", "parallel")),
        name="wprep",
    )(w_in)


def _inproj_body(x_ref, g1_ref, w_ref, b_ref, gq_ref, gk_ref, cs_ref, wvt_ref, bvt_ref, wst_ref, bst_ref,
                 *out_refs):
    x = x_ref[...]
    h = _rms(x, g1_ref[...]).astype(BF16)

    def proj(off, width):
        return _dot(h, w_ref[0, :, off:off + width]) + b_ref[:, off:off + width]

    for gi in range(N_GROUPS):
        z = proj(gi * BW, BW)
        o_ref = out_refs[gi]
        if gi in (G_FXQ, G_FXK):
            g = gq_ref[...] if gi == G_FXQ else gk_ref[...]
            scale = FX_DH ** -0.5 * LOG2E if gi == G_FXQ else 1.0
            for hd in range(HEADS):
                zh = z[:, hd * FX_DH:(hd + 1) * FX_DH]
                o_ref[:, hd * FX_DH:(hd + 1) * FX_DH] = (_rms(zh, g) * scale).astype(o_ref.dtype)
        elif gi == G_GQK:
            o_ref[...] = (z * cs_ref[...]).astype(o_ref.dtype)
        else:
            o_ref[...] = z.astype(o_ref.dtype)
    out_refs[N_GROUPS][...] = proj(N_GROUPS * BW, SMALL_W)
    for n, scale in enumerate((ML_DK ** -0.5, 1.0, 1.0)):
        rows = slice(n * BW, (n + 1) * BW)
        zt = _dot_nt(wvt_ref[0, rows, :], h) + bvt_ref[rows, :]
        out_refs[N_GROUPS + 1 + n][...] = (zt * scale).astype(BF16)
    out_refs[N_GROUPS + 1 + len(WT_COLS)][...] = _dot_nt(wst_ref[0], h) + bst_ref[...]


def _inproj(x2, g1, w1, b1, gq, gk, cs, wvt, bvt, wst, bst, *, layer, tm):
    T, D = x2.shape
    const = lambda i: (0, 0)
    lay = lambda i: (layer, 0, 0)
    row = lambda i: (i, 0)
    col = lambda i: (0, i)
    out_shape = [jax.ShapeDtypeStruct((T, BW), dt) for dt in K1_OUT_DTYPES]
    out_shape += [jax.ShapeDtypeStruct((T, SMALL_W), F32)]
    out_shape += [jax.ShapeDtypeStruct((BW, T), BF16) for _ in WT_COLS]
    out_shape += [jax.ShapeDtypeStruct((SMALL_T_ROWS, T), F32)]
    out_specs = [pl.BlockSpec((tm, BW), row) for _ in K1_OUT_DTYPES]
    out_specs += [pl.BlockSpec((tm, SMALL_W), row)]
    out_specs += [pl.BlockSpec((BW, tm), col) for _ in WT_COLS]
    out_specs += [pl.BlockSpec((SMALL_T_ROWS, tm), col)]
    return pl.pallas_call(
        _inproj_body,
        grid=(T // tm,),
        in_specs=[
            pl.BlockSpec((tm, D), row),
            _resident((1, D), const),
            _resident((1, D, N1), lay),
            _resident((1, N1), const),
            _resident((1, FX_DH), const),
            _resident((1, FX_DH), const),
            _resident((1, BW), const),
            _resident((1, NT, D), lay),
            _resident((NT, 1), const),
            _resident((1, SMALL_T_ROWS, D), lay),
            _resident((SMALL_T_ROWS, 1), const),
        ],
        out_specs=out_specs,
        out_shape=out_shape,
        compiler_params=_cparams(("parallel",)),
        name="inproj",
    )(x2, g1, w1, b1, gq, gk, cs, wvt, bvt, wst, bst)


def _fcum_body(f_ref, o_ref):
    ls = _log_sigmoid(f_ref[0, 0]) * LOG2E
    nb = ls.shape[0]
    ii = lax.broadcasted_iota(jnp.int32, (LANES, LANES), 0)
    jj = lax.broadcasted_iota(jnp.int32, (LANES, LANES), 1)
    local = _dot(ls, (ii <= jj).astype(F32), precision=HIGHEST)
    tot = jnp.broadcast_to(local[:, LANES - 1:LANES], (nb, LANES))
    ri = lax.broadcasted_iota(jnp.int32, (nb, nb), 0)
    ci = lax.broadcasted_iota(jnp.int32, (nb, nb), 1)
    offs = _dot((ci < ri).astype(F32), tot, precision=HIGHEST)
    o_ref[0, 0] = local + offs


def _fcum(small_t4):
    _, B, nb, _ = small_t4.shape
    return pl.pallas_call(
        _fcum_body, grid=(B, HEADS),
        in_specs=[pl.BlockSpec((1, 1, nb, LANES), lambda b, h: (SM_FF + h, b, 0, 0))],
        out_specs=pl.BlockSpec((1, 1, nb, LANES), lambda b, h: (b, h, 0, 0)),
        out_shape=jax.ShapeDtypeStruct((B, HEADS, nb, LANES), F32),
        compiler_params=_cparams(("parallel", "parallel")), name="fox_cumsum",
    )(small_t4)


def _fox_body(q_ref, k_ref, vt_ref, f_ref, o_ref, e_ref, st_ref, m_ref, acc_ref, *, tq, tk):
    qi = pl.program_id(2)
    S = e_ref.shape[1]
    nb = tq // tk
    heads = range(FOX_HEADS_PER_STEP)

    @pl.when(qi == 0)
    def _():
        rid = lax.broadcasted_iota(jnp.int32, (LANES, LANES), 0)

        def build(blk, carry):
            s0 = pl.multiple_of(blk * LANES, LANES)
            for hh in heads:
                hi, mid, lo = _split3(-f_ref[0, hh, :, pl.ds(s0, LANES)])
                bc = lambda part: jnp.broadcast_to(part.astype(F32), (LANES, LANES))
                parts = jnp.where(rid == 0, bc(hi), jnp.where(rid == 1, bc(mid), jnp.where(rid == 2, bc(lo), 0.0)))
                e_ref[hh, pl.ds(s0, LANES), :] = parts.T.astype(BF16)
            return carry

        lax.fori_loop(0, S // LANES, build, 0, unroll=2)

    lane = lax.broadcasted_iota(jnp.int32, (tq, LANES), 1)
    sel = jnp.where(lane < 3, 1.0, 0.0).astype(BF16)
    q_aug = [jnp.concatenate([q_ref[0, :, hh * FX_DH:(hh + 1) * FX_DH], sel], axis=1) for hh in heads]
    ones_rows = jnp.ones((BF16_ROWS, tk), BF16)

    def scores(hh, kj, slot, q_lo=0):
        k0 = pl.multiple_of(kj * tk, tk)
        k_aug = jnp.concatenate([k_ref[0, pl.ds(k0, tk), hh * FX_DH:(hh + 1) * FX_DH],
                                 e_ref[hh, pl.ds(k0, tk), :]], axis=1)
        st_ref[hh, slot, :, q_lo:] = _dot_nt(k_aug, q_aug[hh][q_lo:, :])

    def fold(hh, kj, slot, q_lo=0, diagonal=False):
        st = st_ref[hh, slot, :, q_lo:]
        if diagonal:
            key = lax.broadcasted_iota(jnp.int32, st.shape, 0)
            qry = lax.broadcasted_iota(jnp.int32, st.shape, 1)
            st = jnp.where(key <= qry, st, NEG)
        m_old = m_ref[hh, :, q_lo:]
        m_new = jnp.maximum(m_old, jnp.max(st, axis=0, keepdims=True))
        p = jnp.exp2(st - m_new).astype(BF16)
        k0 = pl.multiple_of(kj * tk, tk)
        v_aug = jnp.concatenate([vt_ref[hh * HEAD_DV:(hh + 1) * HEAD_DV, pl.ds(k0, tk)], ones_rows], axis=0)
        acc_ref[hh, :, q_lo:] = jnp.exp2(m_old - m_new) * acc_ref[hh, :, q_lo:] + _dot(v_aug, p)
        m_ref[hh, :, q_lo:] = m_new

    def body(i, carry):
        for j in range(nb):
            for hh in heads:
                scores(hh, nb * i + j + 1, (j + 1) % nb)
                fold(hh, nb * i + j, j)
        return carry

    m_ref[...] = jnp.full(m_ref.shape, NEG, F32)
    acc_ref[...] = jnp.zeros(acc_ref.shape, F32)
    for hh in heads:
        scores(hh, 0, 0)
    lax.fori_loop(0, qi, body, 0)
    for j in range(nb):
        for hh in heads:
            if j + 1 < nb:
                scores(hh, nb * qi + j + 1, j + 1, q_lo=(j + 1) * tk)
            fold(hh, nb * qi + j, j, q_lo=j * tk, diagonal=True)
    for hh in heads:
        out_t = acc_ref[hh, 0:HEAD_DV, :] / acc_ref[hh, HEAD_DV:HEAD_DV + 1, :]
        o_ref[0, :, hh * FX_DH:(hh + 1) * FX_DH] = out_t.T.astype(o_ref.dtype)


def _fox(q, k, vt, fcum, *, tq, tk):
    B, S, _ = q.shape
    hp = FOX_HEADS_PER_STEP
    qspec = pl.BlockSpec((1, tq, hp * FX_DH), lambda b, h, i: (b, i, h))
    return pl.pallas_call(
        functools.partial(_fox_body, tq=tq, tk=tk),
        grid=(B, HEADS // hp, S // tq),
        in_specs=[qspec,
                  pl.BlockSpec((1, S, hp * FX_DH), lambda b, h, i: (b, 0, h)),
                  pl.BlockSpec((hp * HEAD_DV, S), lambda b, h, i: (h, b)),
                  pl.BlockSpec((1, hp, 1, S), lambda b, h, i: (b, h, 0, 0))],
        out_specs=qspec,
        out_shape=jax.ShapeDtypeStruct((B, S, BW), BF16),
        scratch_shapes=[pltpu.VMEM((hp, S, LANES), BF16), pltpu.VMEM((hp, tq // tk, tk, tq), F32),
                        pltpu.VMEM((hp, 1, tq), F32), pltpu.VMEM((hp, HEAD_DV + BF16_ROWS, tq), F32)],
        compiler_params=_cparams(("parallel", "parallel", "arbitrary")),
        name="fox_attn",
    )(q, k, vt, fcum)


def _mlstm_body(qt_ref, k_ref, vt_ref, sm_ref, smt_ref, o_ref, gn_ref, y_ref, ct_ref, n_ref, m_ref, *, L):
    @pl.when(pl.program_id(1) == 0)
    def _():
        ct_ref[...] = jnp.zeros_like(ct_ref)
        n_ref[...] = jnp.zeros_like(n_ref)
        m_ref[...] = jnp.zeros_like(m_ref)

    g = sm_ref[0]
    gt = smt_ref[...]
    lf = _log_sigmoid(g)
    lft = _log_sigmoid(gt[0:SUBLANES, :])
    ri = lax.broadcasted_iota(jnp.int32, (L, L), 0)
    ci = lax.broadcasted_iota(jnp.int32, (L, L), 1)
    visible = ri <= ci
    b_cols = _dot((ci <= ri).astype(F32), lf, precision=HIGHEST)
    b_rows = _dot(lft, visible.astype(F32), precision=HIGHEST)
    part_row = lax.broadcasted_iota(jnp.int32, (BF16_ROWS, ML_DK), 0)

    for hd in range(HEADS):
        sl = slice(hd * HEAD_DV, (hd + 1) * HEAD_DV)
        qt = qt_ref[sl, :]
        k = k_ref[0, :, sl]
        vt = vt_ref[sl, :]
        b_row = b_rows[SM_MF + hd:SM_MF + hd + 1, :]
        li_row = gt[SM_MI + hd:SM_MI + hd + 1, :]
        u_col = g[:, SM_MI + hd:SM_MI + hd + 1] - b_cols[:, SM_MF + hd:SM_MF + hd + 1]
        b_last = b_row[:, L - 1:L]
        ct_prev = ct_ref[hd]
        n_prev = n_ref[hd]
        m_prev = m_ref[hd][0:1, 0:1]

        dt = jnp.where(visible, b_row + u_col, -jnp.inf)
        inter = b_row + m_prev
        m_t = jnp.maximum(inter, jnp.max(dt, axis=0, keepdims=True))
        st = _dot(k, qt) * jnp.exp(dt - m_t)
        w_inter = jnp.exp(inter - m_t)
        num_t = _dot(vt, st.astype(BF16)) + w_inter * _dot(ct_prev.astype(BF16), qt)
        n_hi, n_mid, n_lo = (jnp.broadcast_to(part.astype(F32), (BF16_ROWS, ML_DK)) for part in _split3(n_prev))
        n_rows = jnp.where(part_row == 0, n_hi, jnp.where(part_row == 1, n_mid, jnp.where(part_row == 2, n_lo, 0.0)))
        qn = _dot(n_rows.astype(BF16), qt)
        den = jnp.sum(st, axis=0, keepdims=True) + w_inter * (qn[0:1, :] + qn[1:2, :] + qn[2:3, :])
        h_t = num_t * (1.0 / jnp.maximum(jnp.abs(den), jnp.exp(-m_t)))
        hn_t = h_t * lax.rsqrt(jnp.mean(h_t * h_t, axis=0, keepdims=True) + EPS)
        y = hn_t.T * gn_ref[:, sl] * _sigmoid(o_ref[0, :, sl])
        y_ref[0, :, sl] = y.astype(y_ref.dtype)

        m_loc = jnp.max(b_last - b_row + li_row, axis=-1, keepdims=True)
        m_new = jnp.maximum(b_last + m_prev, m_loc)
        a = jnp.exp(b_last + m_prev - m_new)
        ke = k.astype(F32) * jnp.exp(b_last + u_col - m_new)
        ct_ref[hd] = a * ct_prev + _dot(vt, ke.astype(BF16))
        n_ref[hd] = a * n_prev + jnp.sum(ke, axis=0, keepdims=True)
        m_ref[hd] = jnp.broadcast_to(m_new, m_ref.shape[1:])


def _mlstm(qt, k, vt, small, small_t, o, gn, *, L):
    B, S, _ = k.shape
    nc = S // L
    blk = pl.BlockSpec((1, L, BW), lambda b, c: (b, c, 0))
    tblk = pl.BlockSpec((BW, L), lambda b, c: (0, b * nc + c))
    return pl.pallas_call(
        functools.partial(_mlstm_body, L=L),
        grid=(B, nc),
        in_specs=[tblk, blk, tblk, pl.BlockSpec((1, L, SMALL_W), lambda b, c: (b, c, 0)),
                  pl.BlockSpec((SMALL_T_ROWS, L), lambda b, c: (0, b * nc + c)), blk,
                  pl.BlockSpec((1, BW), lambda b, c: (0, 0))],
        out_specs=blk,
        out_shape=jax.ShapeDtypeStruct((B, S, BW), BF16),
        scratch_shapes=[pltpu.VMEM((HEADS, HEAD_DV, ML_DK), F32),
                        pltpu.VMEM((HEADS, 1, ML_DK), F32),
                        pltpu.VMEM((HEADS, SUBLANES, LANES), F32)],
        compiler_params=_cparams(("parallel", "arbitrary")),
        name="mlstm",
    )(qt, k, vt, small, small_t, o, gn)


def _gelu_tanh(x):
    return 0.5 * x * (1.0 + jnp.tanh(0.7978845608028654 * (x + 0.044715 * (x * x * x))))


def _lru_body(x_ref, y_ref, cw_ref, cb_ref, wa_ref, ba_ref, wx_ref, bx_ref, lam_ref, o_ref,
              buf_ref, h_ref, *, ts):
    W = x_ref.shape[-1]
    pad = SUBLANES

    @pl.when(pl.program_id(1) == 0)
    def _():
        buf_ref[0:pad, :] = jnp.zeros((pad, W), F32)
        h_ref[...] = jnp.zeros_like(h_ref)

    x = x_ref[0]
    buf_ref[pad:pad + ts, :] = x
    xc = cb_ref[...] + cw_ref[CONV_WIDTH - 1:CONV_WIDTH, :] * x
    for j in range(CONV_WIDTH - 1):
        sh = CONV_WIDTH - 1 - j
        xc = xc + cw_ref[j:j + 1, :] * buf_ref[pad - sh:pad - sh + ts, :]
    buf_ref[0:pad, :] = x[ts - pad:ts, :]

    xb = xc.astype(BF16)
    r = _sigmoid(_dot(xb, wa_ref[...]) + ba_ref[...])
    ig = _sigmoid(_dot(xb, wx_ref[...]) + bx_ref[...])
    log_a = (-LRU_C) * r * _softplus(-lam_ref[...])
    a = jnp.exp(log_a)
    u = jnp.sqrt(-jnp.tanh(log_a) * (a * a + 1.0)) * (ig * xc)

    rid = lax.broadcasted_iota(jnp.int32, (ts, W), 0) & (SUBLANES - 1)
    sh = 1
    while sh < SUBLANES:
        ok = rid >= sh
        a_s = jnp.where(ok, pltpu.roll(a, sh, axis=0), 1.0)
        u_s = jnp.where(ok, pltpu.roll(u, sh, axis=0), 0.0)
        u = u + a * u_s
        a = a * a_s
        sh *= 2

    carry = h_ref[...]
    rows = []
    for gidx in range(ts // SUBLANES):
        sl = slice(gidx * SUBLANES, (gidx + 1) * SUBLANES)
        hg = u[sl, :] + a[sl, :] * carry
        rows.append(hg)
        carry = hg[SUBLANES - 1:SUBLANES, :]
    h_ref[...] = carry
    hs = jnp.concatenate(rows, axis=0)
    o_ref[0] = (hs * _gelu_tanh(y_ref[0])).astype(o_ref.dtype)


def _lru(xb, yb, cw, cb, wa, ba, wx, bx, lam, *, ts):
    B, S, W = xb.shape
    blk = pl.BlockSpec((1, ts, W), lambda b, c: (b, c, 0))
    vec = pl.BlockSpec((1, W), lambda b, c: (0, 0))
    mat = pl.BlockSpec((W, W), lambda b, c: (0, 0))
    return pl.pallas_call(
        functools.partial(_lru_body, ts=ts),
        grid=(B, S // ts),
        in_specs=[blk, blk, pl.BlockSpec((CONV_WIDTH, W), lambda b, c: (0, 0)), vec, mat, vec, mat, vec, vec],
        out_specs=blk,
        out_shape=jax.ShapeDtypeStruct((B, S, W), BF16),
        scratch_shapes=[pltpu.VMEM((ts + SUBLANES, W), F32), pltpu.VMEM((1, W), F32)],
        compiler_params=_cparams(("parallel", "arbitrary")),
        name="rglru",
    )(xb, yb, cw, cb, wa, ba, wx, bx, lam)


def _gla_body(qk_ref, v_ref, g_ref, sm_ref, wal_ref, bal_ref, gn_ref, tril_ref, y_ref, s_ref, *, ts):
    C = GLA_CHUNK
    QW = HEADS * GLA_DK
    nchunk = ts // C

    @pl.when(pl.program_id(1) == 0)
    def _():
        s_ref[...] = jnp.zeros_like(s_ref)

    la = _log_sigmoid(_dot(sm_ref[0], wal_ref[...], precision=HIGHEST) + bal_ref[...]) / GLA_TAU
    parts = jnp.concatenate(_split3(la), axis=1)
    r3 = _dot(tril_ref[...], parts)
    bc = r3[:, 0:QW] + r3[:, QW:2 * QW] + r3[:, 2 * QW:3 * QW]
    b_last = jnp.concatenate(
        [jnp.broadcast_to(bc[(c + 1) * C - 1:(c + 1) * C, :], (C, QW)) for c in range(nchunk)], axis=0)
    qk = qk_ref[0]
    q_t = (qk[:, 0:QW] * jnp.exp(bc)).astype(BF16)
    k = qk[:, QW:2 * QW]
    k_t = (k * jnp.exp(-bc)).astype(BF16)
    k_e = (k * jnp.exp(b_last - bc)).astype(BF16)
    dec = jnp.exp(b_last)

    lane = lax.broadcasted_iota(jnp.int32, (ts, LANES), 1)
    cr = lax.broadcasted_iota(jnp.int32, (C, C), 0)
    cc = lax.broadcasted_iota(jnp.int32, (C, C), 1)
    causal = cc <= cr
    zero = jnp.zeros((), BF16)
    q_h, kt_h, ke_h, dec_h = [], [], [], []
    for hd in range(HEADS):
        pair = slice((hd // 2) * LANES, (hd // 2 + 1) * LANES)
        mine = (lane >= GLA_DK) if hd % 2 else (lane < GLA_DK)
        q_h.append(q_t[:, pair])
        kt_h.append(jnp.where(mine, k_t[:, pair], zero))
        ke_h.append(jnp.where(mine, k_e[:, pair], zero))
        dec_h.append(dec[:, pair])
    st = [s_ref[hd] for hd in range(HEADS)]
    for c in range(nchunk):
        rows = slice(c * C, (c + 1) * C)
        for hd in range(HEADS):
            vs = slice(hd * HEAD_DV, (hd + 1) * HEAD_DV)
            v = v_ref[0, rows, vs]
            attn = jnp.where(causal, _dot_nt(q_h[hd][rows], kt_h[hd][rows]), 0.0)
            o = _dot(attn.astype(BF16), v) + _dot_nt(q_h[hd][rows], st[hd].astype(BF16))
            st[hd] = st[hd] * dec_h[hd][c * C:c * C + 1, :] + _dot_tn(v, ke_h[hd][rows])
            gg = g_ref[0, rows, vs]
            y = _rms(o, gn_ref[:, vs]) * (gg * _sigmoid(gg))
            y_ref[0, rows, vs] = y.astype(y_ref.dtype)
    for hd in range(HEADS):
        s_ref[hd] = st[hd]


def _gla(qk, v, g, small, wal, bal, gn, *, ts):
    B, S, _ = v.shape
    QW = HEADS * GLA_DK
    blk = pl.BlockSpec((1, ts, BW), lambda b, c: (b, c, 0))
    const = lambda b, c: (0, 0)
    pos = jnp.arange(ts, dtype=jnp.int32)
    same_chunk = (pos[:, None] // GLA_CHUNK) == (pos[None, :] // GLA_CHUNK)
    tril_blocks = (same_chunk & (pos[None, :] <= pos[:, None])).astype(BF16)
    return pl.pallas_call(
        functools.partial(_gla_body, ts=ts),
        grid=(B, S // ts),
        in_specs=[blk, blk, blk, pl.BlockSpec((1, ts, SMALL_W), lambda b, c: (b, c, 0)),
                  _resident((SMALL_W, QW), const), _resident((1, QW), const),
                  _resident((1, BW), const), _resident((ts, ts), const)],
        out_specs=blk,
        out_shape=jax.ShapeDtypeStruct((B, S, BW), BF16),
        scratch_shapes=[pltpu.VMEM((HEADS, HEAD_DV, LANES), F32)],
        compiler_params=_cparams(("parallel", "arbitrary")),
        name="gla",
    )(qk, v, g, small, wal, bal, gn, tril_blocks)


def _merge_body(x_ref, g1_ref, y0_ref, y1_ref, y2_ref, y3_ref, wg_ref, bg_ref, wb_ref, wo_ref, o_ref):
    x = x_ref[...]
    D = x.shape[-1]
    h = _rms(x, g1_ref[...]).astype(BF16)
    merged = None
    for n, y_ref in enumerate((y0_ref, y1_ref, y2_ref, y3_ref)):
        gate = _sigmoid(_dot(h, wg_ref[0, :, n * D:(n + 1) * D]) + bg_ref[:, n * D:(n + 1) * D])
        term = gate * _dot(y_ref[...], wb_ref[n])
        merged = term if merged is None else merged + term
    o_ref[...] = x + _dot(merged.astype(BF16), wo_ref[...])


def _merge(x2, g1, ys, wg, bg, wb, wo, *, layer, tm):
    T, D = x2.shape
    row = lambda i: (i, 0)
    const = lambda i: (0, 0)
    yspec = pl.BlockSpec((tm, BW), row)
    return pl.pallas_call(
        _merge_body,
        grid=(T // tm,),
        in_specs=[pl.BlockSpec((tm, D), row), _resident((1, D), const), yspec, yspec, yspec, yspec,
                  _resident((1, D, N_BRANCH * D), lambda i: (layer, 0, 0)), _resident((1, N_BRANCH * D), const),
                  _resident((N_BRANCH, BW, D), lambda i: (0, 0, 0)), _resident((D, D), const)],
        out_specs=pl.BlockSpec((tm, D), row),
        out_shape=jax.ShapeDtypeStruct((T, D), F32),
        compiler_params=_cparams(("parallel",)),
        name="merge",
    )(x2, g1, *ys, wg, bg, wb, wo)


def _mlp_body(x_ref, g2_ref, wu_ref, wd_ref, o_ref, *, nf):
    x = x_ref[...]
    h = _rms(x, g2_ref[...]).astype(BF16)
    dff = wu_ref.shape[1]
    fc = dff // nf
    acc = x
    for c in range(nf):
        up = jnp.maximum(_dot(h, wu_ref[:, c * fc:(c + 1) * fc]), 0.0)
        acc = acc + _dot((up * up).astype(BF16), wd_ref[c * fc:(c + 1) * fc, :])
    o_ref[...] = acc


def _mlp(x2, g2, wu, wd, *, tm, nf=4):
    T, D = x2.shape
    dff = wu.shape[1]
    row = lambda i: (i, 0)
    const = lambda i: (0, 0)
    return pl.pallas_call(
        functools.partial(_mlp_body, nf=nf),
        grid=(T // tm,),
        in_specs=[pl.BlockSpec((tm, D), row), _resident((1, D), const),
                  _resident((D, dff), const), _resident((dff, D), const)],
        out_specs=pl.BlockSpec((tm, D), row),
        out_shape=jax.ShapeDtypeStruct((T, D), F32),
        compiler_params=_cparams(("parallel",)),
        name="mlp",
    )(x2, g2, wu, wd)


def _prep_bias(b_in, D):
    offs, widths = _in_offsets(D)
    take = lambda c: b_in[offs[c]:offs[c] + widths[c]]
    n_small = 3 * HEADS + GLA_RANK
    b_small = jnp.concatenate([take(C_MI), take(C_MF), take(C_FF), take(C_GA), jnp.zeros((SMALL_W - n_small,), F32)])
    b1 = jnp.concatenate([take(c) for c in W1_COLS] + [b_small])[None, :]
    bvt = jnp.concatenate([take(c) for c in WT_COLS])[:, None]
    bst = b_small[:SMALL_T_ROWS, None]
    bg = take(C_GATES)[None, :]
    cs = jnp.ones((1, BW), F32).at[:, 0:HEADS * GLA_DK].set(GLA_DK ** -0.5)
    return b1, bvt, bst, bg, cs


def _block_diag(w):
    nb, n, _ = w.shape
    eye = jnp.eye(nb, dtype=w.dtype)
    return (eye[:, None, :, None] * w[:, :, None, :]).reshape(nb * n, nb * n)


def _layer(x, p, w1, wvt, wst, wg, layer, cfg):
    B, S, D = x.shape
    T = B * S
    x2 = x.reshape(T, D)
    b1, bvt, bst, bg, cs = _prep_bias(p["b_in"], D)
    g1 = p["norm1_g"][None, :]
    outs = _inproj(x2, g1, w1, b1, p["fx_qnorm_g"][None, :], p["fx_knorm_g"][None, :], cs, wvt, bvt, wst, bst,
                   layer=layer, tm=cfg["tm1"])
    ml_qt, ml_vt, fx_vt, small_t = outs[N_GROUPS + 1:]
    (ml_k, ml_o, fx_q, fx_k, r_x, r_y, g_qk, g_v, g_g, small) = [
        o.reshape(B, S, o.shape[-1]) for o in outs[:N_GROUPS + 1]]

    fcum = _fcum(small_t.reshape(SMALL_T_ROWS, B, S // LANES, LANES)).reshape(B, HEADS, 1, S)
    y_fx = _fox(fx_q, fx_k, fx_vt, fcum, tq=cfg["tq"], tk=cfg["tk"])

    y_ml = _mlstm(ml_qt, ml_k, ml_vt, small, small_t, ml_o, p["ml_norm_g"][None, :], L=cfg["ml_chunk"])

    y_lru = _lru(r_x, r_y, p["lru_conv_w"], p["lru_conv_b"][None, :],
                 _block_diag(p["lru_wa"]).astype(BF16), p["lru_ba"][None, :],
                 _block_diag(p["lru_wx"]).astype(BF16), p["lru_bx"][None, :],
                 p["lru_lambda"][None, :], ts=cfg["ts_lru"])

    wal = jnp.zeros((SMALL_W, HEADS * GLA_DK), F32).at[SM_GA:SM_GA + GLA_RANK, :].set(p["gla_w_alpha"])
    y_gla = _gla(g_qk, g_v, g_g, small, wal, p["gla_b_alpha"][None, :],
                 p["gla_norm_g"][None, :], ts=cfg["ts_gla"])

    ys = [y.reshape(T, BW) for y in (y_ml, y_fx, y_lru, y_gla)]
    x2 = _merge(x2, g1, ys, wg, bg, p["w_branch"].astype(BF16), p["w_o"].astype(BF16), layer=layer, tm=cfg["tm3"])
    x2 = _mlp(x2, p["norm2_g"][None, :], p["w_up"].astype(BF16), p["w_down"].astype(BF16), tm=cfg["tm4"])
    return x2.reshape(B, S, D)


def _config(S):
    return dict(tm1=512, tq=min(1024, S), tk=256, ml_chunk=min(256, S), ts_lru=min(512, S), ts_gla=min(512, S),
                tm3=512, tm4=512)


def kernel(x, norm1_g, w_in, b_in, ml_norm_g, fx_qnorm_g, fx_knorm_g, lru_conv_w, lru_conv_b, lru_wa, lru_ba,
           lru_wx, lru_bx, lru_lambda, gla_w_alpha, gla_b_alpha, gla_norm_g, w_branch, w_o, norm2_g, w_up,
           w_down):
    stacked = dict(norm1_g=norm1_g, b_in=b_in, ml_norm_g=ml_norm_g, fx_qnorm_g=fx_qnorm_g,
                   fx_knorm_g=fx_knorm_g, lru_conv_w=lru_conv_w, lru_conv_b=lru_conv_b, lru_wa=lru_wa,
                   lru_ba=lru_ba, lru_wx=lru_wx, lru_bx=lru_bx, lru_lambda=lru_lambda,
                   gla_w_alpha=gla_w_alpha, gla_b_alpha=gla_b_alpha, gla_norm_g=gla_norm_g,
                   w_branch=w_branch, w_o=w_o, norm2_g=norm2_g, w_up=w_up, w_down=w_down)
    cfg = _config(x.shape[1])
    w1, wvt, wst, wg = _wprep(w_in)
    for layer in range(norm1_g.shape[0]):
        x = _layer(x, {name: val[layer] for name, val in stacked.items()}, w1, wvt, wst, wg, layer, cfg)
    return x
```

```python
import functools
import math

import jax
import jax.numpy as jnp
from jax import lax
from jax.experimental import pallas as pl
from jax.experimental.pallas import tpu as pltpu

F32 = jnp.float32
BF16 = jnp.bfloat16
HIGHEST = lax.Precision.HIGHEST

N_BRANCH = 4
BRANCH_WIDTH = 512
BW = BRANCH_WIDTH
HEADS = 4
ML_DK = 128
FX_DH = 128
HEAD_DV = 128
CONV_WIDTH = 4
LRU_C = 8.0
GLA_DK = 64
GLA_RANK = 16
GLA_TAU = 16.0
GLA_CHUNK = 64
FOX_HEADS_PER_STEP = 2
GLA_BATCH_PER_STEP = 2
ML_BATCH_PER_STEP = 2
EPS = 1e-6
LOG2E = math.log2(math.e)

LANES = 128
SUBLANES = 8
BF16_ROWS = 16
VMEM_LIMIT = 56 * 1024 * 1024

SM_MI, SM_MF, SM_FF, SM_GA = 0, 4, 8, 12
SMALL_W = LANES
SMALL_T_ROWS = 16
NEG = -1e30


def _cparams(sem):
    return pltpu.CompilerParams(dimension_semantics=sem, vmem_limit_bytes=VMEM_LIMIT)


def _resident(shape, index_map):
    return pl.BlockSpec(shape, index_map, pipeline_mode=pl.Buffered(1))


def _log_sigmoid(x):
    return jnp.minimum(x, 0.0) - jnp.log1p(jnp.exp(-jnp.abs(x)))


def _softplus(x):
    return jnp.maximum(x, 0.0) + jnp.log1p(jnp.exp(-jnp.abs(x)))


def _sigmoid(x):
    return 0.5 * jnp.tanh(0.5 * x) + 0.5


def _rms(x, g):
    return x * lax.rsqrt(jnp.mean(x * x, axis=-1, keepdims=True) + EPS) * g


def _dot(a, b, **kw):
    return jnp.dot(a, b, preferred_element_type=F32, **kw)


def _dot_nt(a, b, **kw):
    return lax.dot_general(a, b, (((1,), (1,)), ((), ())), preferred_element_type=F32, **kw)


def _dot_tn(a, b, **kw):
    return lax.dot_general(a, b, (((0,), (0,)), ((), ())), preferred_element_type=F32, **kw)


def _split3(x):
    hi = x.astype(BF16)
    r1 = x - hi.astype(F32)
    mid = r1.astype(BF16)
    lo = (r1 - mid.astype(F32)).astype(BF16)
    return hi, mid, lo


def _in_offsets(D):
    widths = (BW, BW, BW, BW, HEADS, HEADS, BW, BW, BW, HEADS, BW, BW,
              HEADS * GLA_DK, HEADS * GLA_DK, BW, BW, GLA_RANK, N_BRANCH * D)
    offs, acc = [], 0
    for w in widths:
        offs.append(acc)
        acc += w
    return offs, widths


(C_MQ, C_MK, C_MV, C_MO, C_MI, C_MF, C_FQ, C_FK, C_FV, C_FF, C_RX, C_RY,
 C_GQ, C_GK, C_GV, C_GG, C_GA, C_GATES) = range(18)
G_MLK, G_MLO, G_FXQ, G_FXK, G_RX, G_RY, G_GQK, G_GV, G_GG = range(9)
N_GROUPS = 9
K1_OUT_DTYPES = (BF16, F32, BF16, BF16, F32, F32, F32, BF16, F32)
W1_COLS = (C_MK, C_MO, C_FQ, C_FK, C_RX, C_RY, C_GQ, C_GK, C_GV, C_GG)
N1 = N_GROUPS * BW + SMALL_W
WT_COLS = (C_MQ, C_MV, C_FV)
NT = len(WT_COLS) * BW


def _wprep_body(w_ref, w1_ref, wvt_ref, wst_ref, wg_ref, *, D):
    offs, widths = _in_offsets(D)
    n_in = offs[-1] + widths[-1]

    def cols(c):
        a0 = (offs[c] // LANES) * LANES
        sh = offs[c] - a0
        end = min(a0 + widths[c] + (LANES if sh else 0), n_in)
        return w_ref[0, :, a0:end][:, sh:sh + widths[c]]

    dst = 0
    for c in W1_COLS:
        w1_ref[0, :, dst:dst + widths[c]] = cols(c).astype(BF16)
        dst += widths[c]

    def window(c):
        a0 = (offs[c] // LANES) * LANES
        return w_ref[0, :, a0:a0 + LANES]

    lane = lax.broadcasted_iota(jnp.int32, (w_ref.shape[1], LANES), 1)
    small = jnp.where(lane < SM_FF, window(C_MI),
                      jnp.where(lane < SM_GA, window(C_FF),
                                jnp.where(lane < SM_GA + GLA_RANK, window(C_GA), 0.0)))
    w1_ref[0, :, dst:dst + SMALL_W] = small.astype(BF16)
    wst_ref[0] = small.T[0:SMALL_T_ROWS, :].astype(BF16)
    for n, c in enumerate(WT_COLS):
        wvt_ref[0, n * BW:(n + 1) * BW, :] = cols(c).T.astype(BF16)
    wg_ref[0] = cols(C_GATES).astype(BF16)


def _wprep(w_in, *, rb=128):
    depth, D, n_in = w_in.shape
    offs, _ = _in_offsets(D)
    assert offs[C_MF] == offs[C_MI] + HEADS and offs[C_MI] % LANES == SM_MI
    assert offs[C_FF] % LANES == SM_FF and offs[C_GA] % LANES == SM_GA
    return pl.pallas_call(
        functools.partial(_wprep_body, D=D),
        grid=(depth, D // rb),
        in_specs=[pl.BlockSpec((1, rb, n_in), lambda l, i: (l, i, 0))],
        out_specs=[pl.BlockSpec((1, rb, N1), lambda l, i: (l, i, 0)),
                   pl.BlockSpec((1, NT, rb), lambda l, i: (l, 0, i)),
                   pl.BlockSpec((1, SMALL_T_ROWS, rb), lambda l, i: (l, 0, i)),
                   pl.BlockSpec((1, rb, N_BRANCH * D), lambda l, i: (l, i, 0))],
        out_shape=[jax.ShapeDtypeStruct((depth, D, N1), BF16),
                   jax.ShapeDtypeStruct((depth, NT, D), BF16),
                   jax.ShapeDtypeStruct((depth, SMALL_T_ROWS, D), BF16),
                   jax.ShapeDtypeStruct((depth, D, N_BRANCH * D), BF16)],
        compiler_params=_cparams(("parallel", "parallel")),
        name="wprep",
    )(w_in)


def _inproj_body(x_ref, g1_ref, w_ref, b_ref, gq_ref, gk_ref, cs_ref, wvt_ref, bvt_ref, wst_ref, bst_ref,
                 *out_refs):
    x = x_ref[...]
    h = _rms(x, g1_ref[...]).astype(BF16)

    def proj(off, width):
        return _dot(h, w_ref[0, :, off:off + width]) + b_ref[:, off:off + width]

    for gi in range(N_GROUPS):
        z = proj(gi * BW, BW)
        o_ref = out_refs[gi]
        if gi in (G_FXQ, G_FXK):
            g = gq_ref[...] if gi == G_FXQ else gk_ref[...]
            scale = FX_DH ** -0.5 * LOG2E if gi == G_FXQ else 1.0
            for hd in range(HEADS):
                zh = z[:, hd * FX_DH:(hd + 1) * FX_DH]
                o_ref[:, hd * FX_DH:(hd + 1) * FX_DH] = (_rms(zh, g) * scale).astype(o_ref.dtype)
        elif gi == G_GQK:
            o_ref[...] = (z * cs_ref[...]).astype(o_ref.dtype)
        else:
            o_ref[...] = z.astype(o_ref.dtype)
    out_refs[N_GROUPS][...] = proj(N_GROUPS * BW, SMALL_W)
    for n, scale in enumerate((ML_DK ** -0.5, 1.0, 1.0)):
        rows = slice(n * BW, (n + 1) * BW)
        zt = _dot_nt(wvt_ref[0, rows, :], h) + bvt_ref[rows, :]
        out_refs[N_GROUPS + 1 + n][0] = (zt * scale).astype(BF16)
    out_refs[N_GROUPS + 1 + len(WT_COLS)][0] = _dot_nt(wst_ref[0], h) + bst_ref[...]


def _inproj(x2, g1, w1, b1, gq, gk, cs, wvt, bvt, wst, bst, *, layer, tm, seq):
    T, D = x2.shape
    tps = seq // tm
    const = lambda i: (0, 0)
    lay = lambda i: (layer, 0, 0)
    row = lambda i: (i, 0)
    col = lambda i: (i // tps, 0, i % tps)
    out_shape = [jax.ShapeDtypeStruct((T, BW), dt) for dt in K1_OUT_DTYPES]
    out_shape += [jax.ShapeDtypeStruct((T, SMALL_W), F32)]
    out_shape += [jax.ShapeDtypeStruct((T // seq, BW, seq), BF16) for _ in WT_COLS]
    out_shape += [jax.ShapeDtypeStruct((T // seq, SMALL_T_ROWS, seq), F32)]
    out_specs = [pl.BlockSpec((tm, BW), row) for _ in K1_OUT_DTYPES]
    out_specs += [pl.BlockSpec((tm, SMALL_W), row)]
    out_specs += [pl.BlockSpec((1, BW, tm), col) for _ in WT_COLS]
    out_specs += [pl.BlockSpec((1, SMALL_T_ROWS, tm), col)]
    return pl.pallas_call(
        _inproj_body,
        grid=(T // tm,),
        in_specs=[
            pl.BlockSpec((tm, D), row),
            _resident((1, D), const),
            _resident((1, D, N1), lay),
            _resident((1, N1), const),
            _resident((1, FX_DH), const),
            _resident((1, FX_DH), const),
            _resident((1, BW), const),
            _resident((1, NT, D), lay),
            _resident((NT, 1), const),
            _resident((1, SMALL_T_ROWS, D), lay),
            _resident((SMALL_T_ROWS, 1), const),
        ],
        out_specs=out_specs,
        out_shape=out_shape,
        compiler_params=_cparams(("parallel",)),
        name="inproj",
    )(x2, g1, w1, b1, gq, gk, cs, wvt, bvt, wst, bst)


def _fcum_body(f_ref, o_ref):
    ls = _log_sigmoid(f_ref[0, 0]) * LOG2E
    nb = ls.shape[0]
    ii = lax.broadcasted_iota(jnp.int32, (LANES, LANES), 0)
    jj = lax.broadcasted_iota(jnp.int32, (LANES, LANES), 1)
    local = _dot(ls, (ii <= jj).astype(F32), precision=HIGHEST)
    tot = jnp.broadcast_to(local[:, LANES - 1:LANES], (nb, LANES))
    ri = lax.broadcasted_iota(jnp.int32, (nb, nb), 0)
    ci = lax.broadcasted_iota(jnp.int32, (nb, nb), 1)
    offs = _dot((ci < ri).astype(F32), tot, precision=HIGHEST)
    o_ref[0, 0] = local + offs


def _fcum(small_t4):
    B, _, nb, _ = small_t4.shape
    return pl.pallas_call(
        _fcum_body, grid=(B, HEADS),
        in_specs=[pl.BlockSpec((1, 1, nb, LANES), lambda b, h: (b, SM_FF + h, 0, 0))],
        out_specs=pl.BlockSpec((1, 1, nb, LANES), lambda b, h: (b, h, 0, 0)),
        out_shape=jax.ShapeDtypeStruct((B, HEADS, nb, LANES), F32),
        compiler_params=_cparams(("parallel", "parallel")), name="fox_cumsum",
    )(small_t4)


def _fox_body(q_ref, k_ref, vt_ref, f_ref, o_ref, e_ref, st_ref, m_ref, acc_ref, *, tq, tk):
    qi = pl.program_id(2)
    S = e_ref.shape[1]
    nb = tq // tk
    heads = range(FOX_HEADS_PER_STEP)

    @pl.when(qi == 0)
    def _():
        rid = lax.broadcasted_iota(jnp.int32, (LANES, LANES), 0)

        def build(blk, carry):
            s0 = pl.multiple_of(blk * LANES, LANES)
            for hh in heads:
                hi, mid, lo = _split3(-f_ref[0, hh, :, pl.ds(s0, LANES)])
                bc = lambda part: jnp.broadcast_to(part.astype(F32), (LANES, LANES))
                parts = jnp.where(rid == 0, bc(hi), jnp.where(rid == 1, bc(mid), jnp.where(rid == 2, bc(lo), 0.0)))
                e_ref[hh, pl.ds(s0, LANES), :] = parts.T.astype(BF16)
            return carry

        lax.fori_loop(0, S // LANES, build, 0, unroll=2)

    lane = lax.broadcasted_iota(jnp.int32, (tq, LANES), 1)
    sel = jnp.where(lane < 3, 1.0, 0.0).astype(BF16)
    q_aug = [jnp.concatenate([q_ref[0, :, hh * FX_DH:(hh + 1) * FX_DH], sel], axis=1) for hh in heads]
    ones_rows = jnp.ones((BF16_ROWS, tk), BF16)

    def scores(hh, kj, slot, q_lo=0):
        k0 = pl.multiple_of(kj * tk, tk)
        k_aug = jnp.concatenate([k_ref[0, pl.ds(k0, tk), hh * FX_DH:(hh + 1) * FX_DH],
                                 e_ref[hh, pl.ds(k0, tk), :]], axis=1)
        st_ref[hh, slot, :, q_lo:] = _dot_nt(k_aug, q_aug[hh][q_lo:, :])

    def fold(hh, kj, slot, q_lo=0, diagonal=False):
        st = st_ref[hh, slot, :, q_lo:]
        if diagonal:
            key = lax.broadcasted_iota(jnp.int32, st.shape, 0)
            qry = lax.broadcasted_iota(jnp.int32, st.shape, 1)
            st = jnp.where(key <= qry, st, NEG)
        m_old = m_ref[hh, :, q_lo:]
        m_new = jnp.maximum(m_old, jnp.max(st, axis=0, keepdims=True))
        p = jnp.exp2(st - m_new).astype(BF16)
        k0 = pl.multiple_of(kj * tk, tk)
        v_aug = jnp.concatenate([vt_ref[0, hh * HEAD_DV:(hh + 1) * HEAD_DV, pl.ds(k0, tk)], ones_rows], axis=0)
        acc_ref[hh, :, q_lo:] = jnp.exp2(m_old - m_new) * acc_ref[hh, :, q_lo:] + _dot(v_aug, p)
        m_ref[hh, :, q_lo:] = m_new

    def body(i, carry):
        for j in range(nb):
            for hh in heads:
                scores(hh, nb * i + j + 1, (j + 1) % nb)
                fold(hh, nb * i + j, j)
        return carry

    m_ref[...] = jnp.full(m_ref.shape, NEG, F32)
    acc_ref[...] = jnp.zeros(acc_ref.shape, F32)
    for hh in heads:
        scores(hh, 0, 0)
    lax.fori_loop(0, qi, body, 0)
    for j in range(nb):
        for hh in heads:
            if j + 1 < nb:
                scores(hh, nb * qi + j + 1, j + 1, q_lo=(j + 1) * tk)
            fold(hh, nb * qi + j, j, q_lo=j * tk, diagonal=True)
    for hh in heads:
        out_t = acc_ref[hh, 0:HEAD_DV, :] / acc_ref[hh, HEAD_DV:HEAD_DV + 1, :]
        o_ref[0, :, hh * FX_DH:(hh + 1) * FX_DH] = out_t.T.astype(o_ref.dtype)


def _fox(q, k, vt, fcum, *, tq, tk):
    B, S, _ = q.shape
    hp = FOX_HEADS_PER_STEP
    qspec = pl.BlockSpec((1, tq, hp * FX_DH), lambda b, h, i: (b, i, h))
    return pl.pallas_call(
        functools.partial(_fox_body, tq=tq, tk=tk),
        grid=(B, HEADS // hp, S // tq),
        in_specs=[qspec,
                  pl.BlockSpec((1, S, hp * FX_DH), lambda b, h, i: (b, 0, h)),
                  pl.BlockSpec((1, hp * HEAD_DV, S), lambda b, h, i: (b, h, 0)),
                  pl.BlockSpec((1, hp, 1, S), lambda b, h, i: (b, h, 0, 0))],
        out_specs=qspec,
        out_shape=jax.ShapeDtypeStruct((B, S, BW), BF16),
        scratch_shapes=[pltpu.VMEM((hp, S, LANES), BF16), pltpu.VMEM((hp, tq // tk, tk, tq), F32),
                        pltpu.VMEM((hp, 1, tq), F32), pltpu.VMEM((hp, HEAD_DV + BF16_ROWS, tq), F32)],
        compiler_params=_cparams(("parallel", "parallel", "arbitrary")),
        name="fox_attn",
    )(q, k, vt, fcum)


def _mlstm_body(qt_ref, k_ref, vt_ref, sm_ref, smt_ref, o_ref, gn_ref, y_ref, ct_ref, n_ref, m_ref, *, L):
    @pl.when(pl.program_id(1) == 0)
    def _():
        ct_ref[...] = jnp.zeros_like(ct_ref)
        n_ref[...] = jnp.zeros_like(n_ref)
        m_ref[...] = jnp.zeros_like(m_ref)

    nbat = k_ref.shape[0]
    ri = lax.broadcasted_iota(jnp.int32, (L, L), 0)
    ci = lax.broadcasted_iota(jnp.int32, (L, L), 1)
    visible = ri <= ci
    tri_down = (ci <= ri).astype(F32)
    tri_right = visible.astype(F32)
    part_row = lax.broadcasted_iota(jnp.int32, (BF16_ROWS, ML_DK), 0)
    gates = []
    for bb in range(nbat):
        g = sm_ref[bb]
        gt = smt_ref[bb]
        b_cols = _dot(tri_down, _log_sigmoid(g), precision=HIGHEST)
        b_rows = _dot(_log_sigmoid(gt[0:SUBLANES, :]), tri_right, precision=HIGHEST)
        gates.append((g, gt, b_cols, b_rows))

    for bb, hd in [(bb, hd) for hd in range(HEADS) for bb in range(nbat)]:
        g, gt, b_cols, b_rows = gates[bb]
        sidx = bb * HEADS + hd
        sl = slice(hd * HEAD_DV, (hd + 1) * HEAD_DV)
        qt = qt_ref[bb, sl, :]
        k = k_ref[bb, :, sl]
        vt = vt_ref[bb, sl, :]
        b_row = b_rows[SM_MF + hd:SM_MF + hd + 1, :]
        li_row = gt[SM_MI + hd:SM_MI + hd + 1, :]
        u_col = g[:, SM_MI + hd:SM_MI + hd + 1] - b_cols[:, SM_MF + hd:SM_MF + hd + 1]
        b_last = b_row[:, L - 1:L]
        ct_prev = ct_ref[sidx]
        n_prev = n_ref[sidx]
        m_prev = m_ref[sidx][0:1, 0:1]

        dt = jnp.where(visible, b_row + u_col, -jnp.inf)
        inter = b_row + m_prev
        m_t = jnp.maximum(inter, jnp.max(dt, axis=0, keepdims=True))
        st = _dot(k, qt) * jnp.exp(dt - m_t)
        w_inter = jnp.exp(inter - m_t)
        num_t = _dot(vt, st.astype(BF16)) + w_inter * _dot(ct_prev.astype(BF16), qt)
        n_hi, n_mid, n_lo = (jnp.broadcast_to(part.astype(F32), (BF16_ROWS, ML_DK)) for part in _split3(n_prev))
        n_rows = jnp.where(part_row == 0, n_hi, jnp.where(part_row == 1, n_mid, jnp.where(part_row == 2, n_lo, 0.0)))
        qn = _dot(n_rows.astype(BF16), qt)
        den = jnp.sum(st, axis=0, keepdims=True) + w_inter * (qn[0:1, :] + qn[1:2, :] + qn[2:3, :])
        h_t = num_t * (1.0 / jnp.maximum(jnp.abs(den), jnp.exp(-m_t)))
        hn_t = h_t * lax.rsqrt(jnp.mean(h_t * h_t, axis=0, keepdims=True) + EPS)
        y = hn_t.T * gn_ref[:, sl] * _sigmoid(o_ref[bb, :, sl])
        y_ref[bb, :, sl] = y.astype(y_ref.dtype)

        m_loc = jnp.max(b_last - b_row + li_row, axis=-1, keepdims=True)
        m_new = jnp.maximum(b_last + m_prev, m_loc)
        a = jnp.exp(b_last + m_prev - m_new)
        ke = k.astype(F32) * jnp.exp(b_last + u_col - m_new)
        ct_ref[sidx] = a * ct_prev + _dot(vt, ke.astype(BF16))
        n_ref[sidx] = a * n_prev + jnp.sum(ke, axis=0, keepdims=True)
        m_ref[sidx] = jnp.broadcast_to(m_new, m_ref.shape[1:])


def _mlstm(qt, k, vt, small, small_t, o, gn, *, L):
    B, S, _ = k.shape
    nbat = ML_BATCH_PER_STEP if B % ML_BATCH_PER_STEP == 0 else 1
    blk = pl.BlockSpec((nbat, L, BW), lambda b, c: (b, c, 0))
    tblk = pl.BlockSpec((nbat, BW, L), lambda b, c: (b, 0, c))
    return pl.pallas_call(
        functools.partial(_mlstm_body, L=L),
        grid=(B // nbat, S // L),
        in_specs=[tblk, blk, tblk, pl.BlockSpec((nbat, L, SMALL_W), lambda b, c: (b, c, 0)),
                  pl.BlockSpec((nbat, SMALL_T_ROWS, L), lambda b, c: (b, 0, c)), blk,
                  pl.BlockSpec((1, BW), lambda b, c: (0, 0))],
        out_specs=blk,
        out_shape=jax.ShapeDtypeStruct((B, S, BW), BF16),
        scratch_shapes=[pltpu.VMEM((nbat * HEADS, HEAD_DV, ML_DK), F32),
                        pltpu.VMEM((nbat * HEADS, 1, ML_DK), F32),
                        pltpu.VMEM((nbat * HEADS, SUBLANES, LANES), F32)],
        compiler_params=_cparams(("parallel", "arbitrary")),
        name="mlstm",
    )(qt, k, vt, small, small_t, o, gn)


def _gelu_tanh(x):
    return 0.5 * x * (1.0 + jnp.tanh(0.7978845608028654 * (x + 0.044715 * (x * x * x))))


def _lru_body(x_ref, y_ref, cw_ref, cb_ref, wa_ref, ba_ref, wx_ref, bx_ref, lam_ref, o_ref,
              buf_ref, h_ref, *, ts):
    W = x_ref.shape[-1]
    pad = SUBLANES

    @pl.when(pl.program_id(1) == 0)
    def _():
        buf_ref[0:pad, :] = jnp.zeros((pad, W), F32)
        h_ref[...] = jnp.zeros_like(h_ref)

    x = x_ref[0]
    buf_ref[pad:pad + ts, :] = x
    xc = cb_ref[...] + cw_ref[CONV_WIDTH - 1:CONV_WIDTH, :] * x
    for j in range(CONV_WIDTH - 1):
        sh = CONV_WIDTH - 1 - j
        xc = xc + cw_ref[j:j + 1, :] * buf_ref[pad - sh:pad - sh + ts, :]
    buf_ref[0:pad, :] = x[ts - pad:ts, :]

    xb = xc.astype(BF16)
    r = _sigmoid(_dot(xb, wa_ref[...]) + ba_ref[...])
    ig = _sigmoid(_dot(xb, wx_ref[...]) + bx_ref[...])
    log_a = (-LRU_C) * r * _softplus(-lam_ref[...])
    a = jnp.exp(log_a)
    u = jnp.sqrt(-jnp.tanh(log_a) * (a * a + 1.0)) * (ig * xc)

    ng = ts // SUBLANES
    a = a.reshape(ng, SUBLANES, W)
    u = u.reshape(ng, SUBLANES, W)
    rid = lax.broadcasted_iota(jnp.int32, (ng, SUBLANES, W), 1)
    sh = 1
    while sh < SUBLANES:
        ok = rid >= sh
        a_s = jnp.where(ok, pltpu.roll(a, sh, axis=1), 1.0)
        u_s = jnp.where(ok, pltpu.roll(u, sh, axis=1), 0.0)
        u = u + a * u_s
        a = a * a_s
        sh *= 2

    carry = h_ref[...]
    rows = []
    for gidx in range(ng):
        hg = u[gidx] + a[gidx] * carry
        rows.append(hg)
        carry = hg[SUBLANES - 1:SUBLANES, :]
    h_ref[...] = carry
    hs = jnp.concatenate(rows, axis=0)
    o_ref[0] = (hs * _gelu_tanh(y_ref[0])).astype(o_ref.dtype)


def _lru(xb, yb, cw, cb, wa, ba, wx, bx, lam, *, ts):
    B, S, W = xb.shape
    blk = pl.BlockSpec((1, ts, W), lambda b, c: (b, c, 0))
    vec = pl.BlockSpec((1, W), lambda b, c: (0, 0))
    mat = pl.BlockSpec((W, W), lambda b, c: (0, 0))
    return pl.pallas_call(
        functools.partial(_lru_body, ts=ts),
        grid=(B, S // ts),
        in_specs=[blk, blk, pl.BlockSpec((CONV_WIDTH, W), lambda b, c: (0, 0)), vec, mat, vec, mat, vec, vec],
        out_specs=blk,
        out_shape=jax.ShapeDtypeStruct((B, S, W), BF16),
        scratch_shapes=[pltpu.VMEM((ts + SUBLANES, W), F32), pltpu.VMEM((1, W), F32)],
        compiler_params=_cparams(("parallel", "arbitrary")),
        name="rglru",
    )(xb, yb, cw, cb, wa, ba, wx, bx, lam)


def _gla_body(qk_ref, v_ref, g_ref, sm_ref, wal_ref, bal_ref, gn_ref, tril_ref, y_ref, s_ref, *, ts):
    C = GLA_CHUNK
    QW = HEADS * GLA_DK
    nchunk = ts // C
    nbat = qk_ref.shape[0]

    @pl.when(pl.program_id(1) == 0)
    def _():
        s_ref[...] = jnp.zeros_like(s_ref)

    lane = lax.broadcasted_iota(jnp.int32, (ts, LANES), 1)
    cr = lax.broadcasted_iota(jnp.int32, (C, C), 0)
    cc = lax.broadcasted_iota(jnp.int32, (C, C), 1)
    causal = cc <= cr
    zero = jnp.zeros((), BF16)
    q_h, kt_h, ke_h, dec_h = {}, {}, {}, {}
    for bb in range(nbat):
        la = _log_sigmoid(_dot(sm_ref[bb], wal_ref[...], precision=HIGHEST) + bal_ref[...]) / GLA_TAU
        parts = jnp.concatenate(_split3(la), axis=1)
        r3 = _dot(tril_ref[...], parts)
        bc = r3[:, 0:QW] + r3[:, QW:2 * QW] + r3[:, 2 * QW:3 * QW]
        b_last = jnp.concatenate(
            [jnp.broadcast_to(bc[(c + 1) * C - 1:(c + 1) * C, :], (C, QW)) for c in range(nchunk)], axis=0)
        qk = qk_ref[bb]
        q_t = (qk[:, 0:QW] * jnp.exp(bc)).astype(BF16)
        k = qk[:, QW:2 * QW]
        k_t = (k * jnp.exp(-bc)).astype(BF16)
        k_e = (k * jnp.exp(b_last - bc)).astype(BF16)
        dec = jnp.exp(b_last)
        for hd in range(HEADS):
            pair = slice((hd // 2) * LANES, (hd // 2 + 1) * LANES)
            mine = (lane >= GLA_DK) if hd % 2 else (lane < GLA_DK)
            q_h[bb, hd] = q_t[:, pair]
            kt_h[bb, hd] = jnp.where(mine, k_t[:, pair], zero)
            ke_h[bb, hd] = jnp.where(mine, k_e[:, pair], zero)
            dec_h[bb, hd] = dec[:, pair]
    chains = [(bb, hd) for bb in range(nbat) for hd in range(HEADS)]
    st = {ch: s_ref[ch[0] * HEADS + ch[1]] for ch in chains}
    for c in range(nchunk):
        rows = slice(c * C, (c + 1) * C)
        for ch in chains:
            bb, hd = ch
            vs = slice(hd * HEAD_DV, (hd + 1) * HEAD_DV)
            v = v_ref[bb, rows, vs]
            attn = jnp.where(causal, _dot_nt(q_h[ch][rows], kt_h[ch][rows]), 0.0)
            o = _dot(attn.astype(BF16), v) + _dot_nt(q_h[ch][rows], st[ch].astype(BF16))
            st[ch] = st[ch] * dec_h[ch][c * C:c * C + 1, :] + _dot_tn(v, ke_h[ch][rows])
            gg = g_ref[bb, rows, vs]
            y = _rms(o, gn_ref[:, vs]) * (gg * _sigmoid(gg))
            y_ref[bb, rows, vs] = y.astype(y_ref.dtype)
    for ch in chains:
        s_ref[ch[0] * HEADS + ch[1]] = st[ch]


def _gla(qk, v, g, small, wal, bal, gn, *, ts):
    B, S, _ = v.shape
    QW = HEADS * GLA_DK
    nbat = GLA_BATCH_PER_STEP if B % GLA_BATCH_PER_STEP == 0 else 1
    blk = pl.BlockSpec((nbat, ts, BW), lambda b, c: (b, c, 0))
    const = lambda b, c: (0, 0)
    pos = jnp.arange(ts, dtype=jnp.int32)
    same_chunk = (pos[:, None] // GLA_CHUNK) == (pos[None, :] // GLA_CHUNK)
    tril_blocks = (same_chunk & (pos[None, :] <= pos[:, None])).astype(BF16)
    return pl.pallas_call(
        functools.partial(_gla_body, ts=ts),
        grid=(B // nbat, S // ts),
        in_specs=[blk, blk, blk, pl.BlockSpec((nbat, ts, SMALL_W), lambda b, c: (b, c, 0)),
                  _resident((SMALL_W, QW), const), _resident((1, QW), const),
                  _resident((1, BW), const), _resident((ts, ts), const)],
        out_specs=blk,
        out_shape=jax.ShapeDtypeStruct((B, S, BW), BF16),
        scratch_shapes=[pltpu.VMEM((nbat * HEADS, HEAD_DV, LANES), F32)],
        compiler_params=_cparams(("parallel", "arbitrary")),
        name="gla",
    )(qk, v, g, small, wal, bal, gn, tril_blocks)


def _merge_body(x_ref, g1_ref, y0_ref, y1_ref, y2_ref, y3_ref, wg_ref, bg_ref, wb_ref, wo_ref, o_ref):
    x = x_ref[...]
    D = x.shape[-1]
    h = _rms(x, g1_ref[...]).astype(BF16)
    merged = None
    for n, y_ref in enumerate((y0_ref, y1_ref, y2_ref, y3_ref)):
        gate = _sigmoid(_dot(h, wg_ref[0, :, n * D:(n + 1) * D]) + bg_ref[:, n * D:(n + 1) * D])
        term = gate * _dot(y_ref[...], wb_ref[n])
        merged = term if merged is None else merged + term
    o_ref[...] = x + _dot(merged.astype(BF16), wo_ref[...])


def _merge(x2, g1, ys, wg, bg, wb, wo, *, layer, tm):
    T, D = x2.shape
    row = lambda i: (i, 0)
    const = lambda i: (0, 0)
    yspec = pl.BlockSpec((tm, BW), row)
    return pl.pallas_call(
        _merge_body,
        grid=(T // tm,),
        in_specs=[pl.BlockSpec((tm, D), row), _resident((1, D), const), yspec, yspec, yspec, yspec,
                  _resident((1, D, N_BRANCH * D), lambda i: (layer, 0, 0)), _resident((1, N_BRANCH * D), const),
                  _resident((N_BRANCH, BW, D), lambda i: (0, 0, 0)), _resident((D, D), const)],
        out_specs=pl.BlockSpec((tm, D), row),
        out_shape=jax.ShapeDtypeStruct((T, D), F32),
        compiler_params=_cparams(("parallel",)),
        name="merge",
    )(x2, g1, *ys, wg, bg, wb, wo)


def _mlp_body(x_ref, g2_ref, wu_ref, wd_ref, o_ref, *, nf):
    x = x_ref[...]
    h = _rms(x, g2_ref[...]).astype(BF16)
    dff = wu_ref.shape[1]
    fc = dff // nf
    acc = x
    for c in range(nf):
        up = jnp.maximum(_dot(h, wu_ref[:, c * fc:(c + 1) * fc]), 0.0)
        acc = acc + _dot((up * up).astype(BF16), wd_ref[c * fc:(c + 1) * fc, :])
    o_ref[...] = acc


def _mlp(x2, g2, wu, wd, *, tm, nf=4):
    T, D = x2.shape
    dff = wu.shape[1]
    row = lambda i: (i, 0)
    const = lambda i: (0, 0)
    return pl.pallas_call(
        functools.partial(_mlp_body, nf=nf),
        grid=(T // tm,),
        in_specs=[pl.BlockSpec((tm, D), row), _resident((1, D), const),
                  _resident((D, dff), const), _resident((dff, D), const)],
        out_specs=pl.BlockSpec((tm, D), row),
        out_shape=jax.ShapeDtypeStruct((T, D), F32),
        compiler_params=_cparams(("parallel",)),
        name="mlp",
    )(x2, g2, wu, wd)


def _prep_bias(b_in, D):
    offs, widths = _in_offsets(D)
    take = lambda c: b_in[offs[c]:offs[c] + widths[c]]
    n_small = 3 * HEADS + GLA_RANK
    b_small = jnp.concatenate([take(C_MI), take(C_MF), take(C_FF), take(C_GA), jnp.zeros((SMALL_W - n_small,), F32)])
    b1 = jnp.concatenate([take(c) for c in W1_COLS] + [b_small])[None, :]
    bvt = jnp.concatenate([take(c) for c in WT_COLS])[:, None]
    bst = b_small[:SMALL_T_ROWS, None]
    bg = take(C_GATES)[None, :]
    cs = jnp.ones((1, BW), F32).at[:, 0:HEADS * GLA_DK].set(GLA_DK ** -0.5)
    return b1, bvt, bst, bg, cs


def _block_diag(w):
    nb, n, _ = w.shape
    eye = jnp.eye(nb, dtype=w.dtype)
    return (eye[:, None, :, None] * w[:, :, None, :]).reshape(nb * n, nb * n)


def _layer(x, p, w1, wvt, wst, wg, layer, cfg):
    B, S, D = x.shape
    T = B * S
    x2 = x.reshape(T, D)
    b1, bvt, bst, bg, cs = _prep_bias(p["b_in"], D)
    g1 = p["norm1_g"][None, :]
    outs = _inproj(x2, g1, w1, b1, p["fx_qnorm_g"][None, :], p["fx_knorm_g"][None, :], cs, wvt, bvt, wst, bst,
                   layer=layer, tm=cfg["tm1"], seq=S)
    ml_qt, ml_vt, fx_vt, small_t = outs[N_GROUPS + 1:]
    (ml_k, ml_o, fx_q, fx_k, r_x, r_y, g_qk, g_v, g_g, small) = [
        o.reshape(B, S, o.shape[-1]) for o in outs[:N_GROUPS + 1]]

    fcum = _fcum(small_t.reshape(B, SMALL_T_ROWS, S // LANES, LANES)).reshape(B, HEADS, 1, S)
    y_fx = _fox(fx_q, fx_k, fx_vt, fcum, tq=cfg["tq"], tk=cfg["tk"])

    y_ml = _mlstm(ml_qt, ml_k, ml_vt, small, small_t, ml_o, p["ml_norm_g"][None, :], L=cfg["ml_chunk"])

    y_lru = _lru(r_x, r_y, p["lru_conv_w"], p["lru_conv_b"][None, :],
                 _block_diag(p["lru_wa"]).astype(BF16), p["lru_ba"][None, :],
                 _block_diag(p["lru_wx"]).astype(BF16), p["lru_bx"][None, :],
                 p["lru_lambda"][None, :], ts=cfg["ts_lru"])

    wal = jnp.zeros((SMALL_W, HEADS * GLA_DK), F32).at[SM_GA:SM_GA + GLA_RANK, :].set(p["gla_w_alpha"])
    y_gla = _gla(g_qk, g_v, g_g, small, wal, p["gla_b_alpha"][None, :],
                 p["gla_norm_g"][None, :], ts=cfg["ts_gla"])

    ys = [y.reshape(T, BW) for y in (y_ml, y_fx, y_lru, y_gla)]
    x2 = _merge(x2, g1, ys, wg, bg, p["w_branch"].astype(BF16), p["w_o"].astype(BF16), layer=layer, tm=cfg["tm3"])
    x2 = _mlp(x2, p["norm2_g"][None, :], p["w_up"].astype(BF16), p["w_down"].astype(BF16), tm=cfg["tm4"])
    return x2.reshape(B, S, D)


def _config(S):
    return dict(tm1=512, tq=min(1024, S), tk=512, ml_chunk=min(256, S), ts_lru=min(512, S), ts_gla=min(512, S),
                tm3=512, tm4=512)


def kernel(x, norm1_g, w_in, b_in, ml_norm_g, fx_qnorm_g, fx_knorm_g, lru_conv_w, lru_conv_b, lru_wa, lru_ba,
           lru_wx, lru_bx, lru_lambda, gla_w_alpha, gla_b_alpha, gla_norm_g, w_branch, w_o, norm2_g, w_up,
           w_down):
    stacked = dict(norm1_g=norm1_g, b_in=b_in, ml_norm_g=ml_norm_g, fx_qnorm_g=fx_qnorm_g,
                   fx_knorm_g=fx_knorm_g, lru_conv_w=lru_conv_w, lru_conv_b=lru_conv_b, lru_wa=lru_wa,
                   lru_ba=lru_ba, lru_wx=lru_wx, lru_bx=lru_bx, lru_lambda=lru_lambda,
                   gla_w_alpha=gla_w_alpha, gla_b_alpha=gla_b_alpha, gla_norm_g=gla_norm_g,
                   w_branch=w_branch, w_o=w_o, norm2_g=norm2_g, w_up=w_up, w_down=w_down)
    cfg = _config(x.shape[1])
    w1, wvt, wst, wg = _wprep(w_in)
    for layer in range(norm1_g.shape[0]):
        x = _layer(x, {name: val[layer] for name, val in stacked.items()}, w1, wvt, wst, wg, layer, cfg)
    return x
```

```python
import functools
import math

import jax
import jax.numpy as jnp
from jax import lax
from jax.experimental import pallas as pl
from jax.experimental.pallas import tpu as pltpu

F32 = jnp.float32
BF16 = jnp.bfloat16
HIGHEST = lax.Precision.HIGHEST

N_BRANCH = 4
BRANCH_WIDTH = 512
BW = BRANCH_WIDTH
HEADS = 4
ML_DK = 128
FX_DH = 128
HEAD_DV = 128
CONV_WIDTH = 4
LRU_C = 8.0
GLA_DK = 64
GLA_RANK = 16
GLA_TAU = 16.0
GLA_CHUNK = 64
FOX_HEADS_PER_STEP = 2
GLA_BATCH_PER_STEP = 4
ML_BATCH_PER_STEP = 4
EPS = 1e-6
LOG2E = math.log2(math.e)

LANES = 128
SUBLANES = 8
BF16_ROWS = 16
VMEM_LIMIT = 56 * 1024 * 1024

SM_MI, SM_MF, SM_FF, SM_GA = 0, 4, 8, 12
SMALL_W = LANES
SMALL_T_ROWS = 16
NEG = -1e30


def _cparams(sem):
    return pltpu.CompilerParams(dimension_semantics=sem, vmem_limit_bytes=VMEM_LIMIT)


def _resident(shape, index_map):
    return pl.BlockSpec(shape, index_map, pipeline_mode=pl.Buffered(1))


def _log_sigmoid(x):
    return jnp.minimum(x, 0.0) - jnp.log1p(jnp.exp(-jnp.abs(x)))


def _softplus(x):
    return jnp.maximum(x, 0.0) + jnp.log1p(jnp.exp(-jnp.abs(x)))


def _sigmoid(x):
    return 0.5 * jnp.tanh(0.5 * x) + 0.5


def _rms(x, g):
    return x * lax.rsqrt(jnp.mean(x * x, axis=-1, keepdims=True) + EPS) * g


def _dot(a, b, **kw):
    return jnp.dot(a, b, preferred_element_type=F32, **kw)


def _dot_nt(a, b, **kw):
    return lax.dot_general(a, b, (((1,), (1,)), ((), ())), preferred_element_type=F32, **kw)


def _dot_tn(a, b, **kw):
    return lax.dot_general(a, b, (((0,), (0,)), ((), ())), preferred_element_type=F32, **kw)


def _split3(x):
    hi = x.astype(BF16)
    r1 = x - hi.astype(F32)
    mid = r1.astype(BF16)
    lo = (r1 - mid.astype(F32)).astype(BF16)
    return hi, mid, lo


def _in_offsets(D):
    widths = (BW, BW, BW, BW, HEADS, HEADS, BW, BW, BW, HEADS, BW, BW,
              HEADS * GLA_DK, HEADS * GLA_DK, BW, BW, GLA_RANK, N_BRANCH * D)
    offs, acc = [], 0
    for w in widths:
        offs.append(acc)
        acc += w
    return offs, widths


(C_MQ, C_MK, C_MV, C_MO, C_MI, C_MF, C_FQ, C_FK, C_FV, C_FF, C_RX, C_RY,
 C_GQ, C_GK, C_GV, C_GG, C_GA, C_GATES) = range(18)
G_MLK, G_MLO, G_FXQ, G_FXK, G_RX, G_RY, G_GQK, G_GV, G_GG = range(9)
N_GROUPS = 9
K1_OUT_DTYPES = (BF16, F32, BF16, BF16, F32, F32, F32, BF16, F32)
W1_COLS = (C_MK, C_MO, C_FQ, C_FK, C_RX, C_RY, C_GQ, C_GK, C_GV, C_GG)
N1 = N_GROUPS * BW + SMALL_W
WT_COLS = (C_MQ, C_MV, C_FV)
NT = len(WT_COLS) * BW


def _wprep_body(w_ref, w1_ref, wvt_ref, wst_ref, wg_ref, *, D):
    offs, widths = _in_offsets(D)
    n_in = offs[-1] + widths[-1]

    def cols(c):
        a0 = (offs[c] // LANES) * LANES
        sh = offs[c] - a0
        end = min(a0 + widths[c] + (LANES if sh else 0), n_in)
        return w_ref[0, :, a0:end][:, sh:sh + widths[c]]

    dst = 0
    for c in W1_COLS:
        w1_ref[0, :, dst:dst + widths[c]] = cols(c).astype(BF16)
        dst += widths[c]

    def window(c):
        a0 = (offs[c] // LANES) * LANES
        return w_ref[0, :, a0:a0 + LANES]

    lane = lax.broadcasted_iota(jnp.int32, (w_ref.shape[1], LANES), 1)
    small = jnp.where(lane < SM_FF, window(C_MI),
                      jnp.where(lane < SM_GA, window(C_FF),
                                jnp.where(lane < SM_GA + GLA_RANK, window(C_GA), 0.0)))
    w1_ref[0, :, dst:dst + SMALL_W] = small.astype(BF16)
    wst_ref[0] = small.T[0:SMALL_T_ROWS, :].astype(BF16)
    for n, c in enumerate(WT_COLS):
        wvt_ref[0, n * BW:(n + 1) * BW, :] = cols(c).T.astype(BF16)
    wg_ref[0] = cols(C_GATES).astype(BF16)


def _wprep(w_in, *, rb=128):
    depth, D, n_in = w_in.shape
    offs, _ = _in_offsets(D)
    assert offs[C_MF] == offs[C_MI] + HEADS and offs[C_MI] % LANES == SM_MI
    assert offs[C_FF] % LANES == SM_FF and offs[C_GA] % LANES == SM_GA
    return pl.pallas_call(
        functools.partial(_wprep_body, D=D),
        grid=(depth, D // rb),
        in_specs=[pl.BlockSpec((1, rb, n_in), lambda l, i: (l, i, 0))],
        out_specs=[pl.BlockSpec((1, rb, N1), lambda l, i: (l, i, 0)),
                   pl.BlockSpec((1, NT, rb), lambda l, i: (l, 0, i)),
                   pl.BlockSpec((1, SMALL_T_ROWS, rb), lambda l, i: (l, 0, i)),
                   pl.BlockSpec((1, rb, N_BRANCH * D), lambda l, i: (l, i, 0))],
        out_shape=[jax.ShapeDtypeStruct((depth, D, N1), BF16),
                   jax.ShapeDtypeStruct((depth, NT, D), BF16),
                   jax.ShapeDtypeStruct((depth, SMALL_T_ROWS, D), BF16),
                   jax.ShapeDtypeStruct((depth, D, N_BRANCH * D), BF16)],
        compiler_params=_cparams(("parallel", "parallel")),
        name="wprep",
    )(w_in)


def _inproj_body(x_ref, g1_ref, w_ref, b_ref, gq_ref, gk_ref, cs_ref, wvt_ref, bvt_ref, wst_ref, bst_ref,
                 *out_refs):
    x = x_ref[...]
    h = _rms(x, g1_ref[...]).astype(BF16)

    def proj(off, width):
        return _dot(h, w_ref[0, :, off:off + width]) + b_ref[:, off:off + width]

    for gi in range(N_GROUPS):
        z = proj(gi * BW, BW)
        o_ref = out_refs[gi]
        if gi in (G_FXQ, G_FXK):
            g = gq_ref[...] if gi == G_FXQ else gk_ref[...]
            scale = FX_DH ** -0.5 * LOG2E if gi == G_FXQ else 1.0
            for hd in range(HEADS):
                zh = z[:, hd * FX_DH:(hd + 1) * FX_DH]
                grp, pos = divmod(hd, FOX_HEADS_PER_STEP)
                o_ref[grp, :, pos * FX_DH:(pos + 1) * FX_DH] = (_rms(zh, g) * scale).astype(o_ref.dtype)
        elif gi == G_GQK:
            o_ref[...] = (z * cs_ref[...]).astype(o_ref.dtype)
        else:
            o_ref[...] = z.astype(o_ref.dtype)
    out_refs[N_GROUPS][...] = proj(N_GROUPS * BW, SMALL_W)
    for n, scale in enumerate((ML_DK ** -0.5, 1.0, 1.0)):
        rows = slice(n * BW, (n + 1) * BW)
        zt = _dot_nt(wvt_ref[0, rows, :], h) + bvt_ref[rows, :]
        out_refs[N_GROUPS + 1 + n][0] = (zt * scale).astype(BF16)
    out_refs[N_GROUPS + 1 + len(WT_COLS)][0] = _dot_nt(wst_ref[0], h) + bst_ref[...]


def _inproj(x2, g1, w1, b1, gq, gk, cs, wvt, bvt, wst, bst, *, layer, tm, seq):
    T, D = x2.shape
    tps = seq // tm
    const = lambda i: (0, 0)
    lay = lambda i: (layer, 0, 0)
    row = lambda i: (i, 0)
    col = lambda i: (i // tps, 0, i % tps)
    out_shape = [jax.ShapeDtypeStruct((T, BW), dt) for dt in K1_OUT_DTYPES]
    out_specs = [pl.BlockSpec((tm, BW), row) for _ in K1_OUT_DTYPES]
    ngrp, gw = HEADS // FOX_HEADS_PER_STEP, FOX_HEADS_PER_STEP * FX_DH
    for gi in (G_FXQ, G_FXK):
        out_shape[gi] = jax.ShapeDtypeStruct((ngrp, T, gw), K1_OUT_DTYPES[gi])
        out_specs[gi] = pl.BlockSpec((ngrp, tm, gw), lambda i: (0, i, 0))
    out_shape += [jax.ShapeDtypeStruct((T, SMALL_W), F32)]
    out_shape += [jax.ShapeDtypeStruct((T // seq, BW, seq), BF16) for _ in WT_COLS]
    out_shape += [jax.ShapeDtypeStruct((T // seq, SMALL_T_ROWS, seq), F32)]
    out_specs += [pl.BlockSpec((tm, SMALL_W), row)]
    out_specs += [pl.BlockSpec((1, BW, tm), col) for _ in WT_COLS]
    out_specs += [pl.BlockSpec((1, SMALL_T_ROWS, tm), col)]
    return pl.pallas_call(
        _inproj_body,
        grid=(T // tm,),
        in_specs=[
            pl.BlockSpec((tm, D), row),
            _resident((1, D), const),
            _resident((1, D, N1), lay),
            _resident((1, N1), const),
            _resident((1, FX_DH), const),
            _resident((1, FX_DH), const),
            _resident((1, BW), const),
            _resident((1, NT, D), lay),
            _resident((NT, 1), const),
            _resident((1, SMALL_T_ROWS, D), lay),
            _resident((SMALL_T_ROWS, 1), const),
        ],
        out_specs=out_specs,
        out_shape=out_shape,
        compiler_params=_cparams(("parallel",)),
        name="inproj",
    )(x2, g1, w1, b1, gq, gk, cs, wvt, bvt, wst, bst)


def _fcum_body(f_ref, o_ref):
    ls = _log_sigmoid(f_ref[0, 0]) * LOG2E
    nb = ls.shape[0]
    ii = lax.broadcasted_iota(jnp.int32, (LANES, LANES), 0)
    jj = lax.broadcasted_iota(jnp.int32, (LANES, LANES), 1)
    local = _dot(ls, (ii <= jj).astype(F32), precision=HIGHEST)
    tot = jnp.broadcast_to(local[:, LANES - 1:LANES], (nb, LANES))
    ri = lax.broadcasted_iota(jnp.int32, (nb, nb), 0)
    ci = lax.broadcasted_iota(jnp.int32, (nb, nb), 1)
    offs = _dot((ci < ri).astype(F32), tot, precision=HIGHEST)
    o_ref[0, 0] = local + offs


def _fcum(small_t4):
    B, _, nb, _ = small_t4.shape
    return pl.pallas_call(
        _fcum_body, grid=(B, HEADS),
        in_specs=[pl.BlockSpec((1, 1, nb, LANES), lambda b, h: (b, SM_FF + h, 0, 0))],
        out_specs=pl.BlockSpec((1, 1, nb, LANES), lambda b, h: (b, h, 0, 0)),
        out_shape=jax.ShapeDtypeStruct((B, HEADS, nb, LANES), F32),
        compiler_params=_cparams(("parallel", "parallel")), name="fox_cumsum",
    )(small_t4)


def _fox_body(q_ref, k_ref, vt_ref, f_ref, o_ref, e_ref, st_ref, m_ref, acc_ref, *, tq, tk):
    qi = pl.program_id(2)
    S = e_ref.shape[1]
    nb = tq // tk
    heads = range(FOX_HEADS_PER_STEP)

    @pl.when(qi == 0)
    def _():
        rid = lax.broadcasted_iota(jnp.int32, (LANES, LANES), 0)

        def build(blk, carry):
            s0 = pl.multiple_of(blk * LANES, LANES)
            for hh in heads:
                hi, mid, lo = _split3(-f_ref[0, hh, :, pl.ds(s0, LANES)])
                bc = lambda part: jnp.broadcast_to(part.astype(F32), (LANES, LANES))
                parts = jnp.where(rid == 0, bc(hi), jnp.where(rid == 1, bc(mid), jnp.where(rid == 2, bc(lo), 0.0)))
                e_ref[hh, pl.ds(s0, LANES), :] = parts.T.astype(BF16)
            return carry

        lax.fori_loop(0, S // LANES, build, 0, unroll=2)

    lane = lax.broadcasted_iota(jnp.int32, (tq, LANES), 1)
    sel = jnp.where(lane < 3, 1.0, 0.0).astype(BF16)
    q_aug = [jnp.concatenate([q_ref[0, 0, :, hh * FX_DH:(hh + 1) * FX_DH], sel], axis=1) for hh in heads]
    ones_rows = jnp.ones((BF16_ROWS, tk), BF16)

    def scores(hh, kj, slot, q_lo=0):
        k0 = pl.multiple_of(kj * tk, tk)
        k_aug = jnp.concatenate([k_ref[0, 0, pl.ds(k0, tk), hh * FX_DH:(hh + 1) * FX_DH],
                                 e_ref[hh, pl.ds(k0, tk), :]], axis=1)
        st_ref[hh, slot, :, q_lo:] = _dot_nt(k_aug, q_aug[hh][q_lo:, :])

    def fold(hh, kj, slot, q_lo=0, diagonal=False):
        st = st_ref[hh, slot, :, q_lo:]
        if diagonal:
            key = lax.broadcasted_iota(jnp.int32, st.shape, 0)
            qry = lax.broadcasted_iota(jnp.int32, st.shape, 1)
            st = jnp.where(key <= qry, st, NEG)
        m_old = m_ref[hh, :, q_lo:]
        m_new = jnp.maximum(m_old, jnp.max(st, axis=0, keepdims=True))
        p = jnp.exp2(st - m_new).astype(BF16)
        k0 = pl.multiple_of(kj * tk, tk)
        v_aug = jnp.concatenate([vt_ref[0, hh * HEAD_DV:(hh + 1) * HEAD_DV, pl.ds(k0, tk)], ones_rows], axis=0)
        acc_ref[hh, :, q_lo:] = jnp.exp2(m_old - m_new) * acc_ref[hh, :, q_lo:] + _dot(v_aug, p)
        m_ref[hh, :, q_lo:] = m_new

    def body(i, carry):
        for j in range(nb):
            for hh in heads:
                scores(hh, nb * i + j + 1, (j + 1) % nb)
                fold(hh, nb * i + j, j)
        return carry

    m_ref[...] = jnp.full(m_ref.shape, NEG, F32)
    acc_ref[...] = jnp.zeros(acc_ref.shape, F32)
    for hh in heads:
        scores(hh, 0, 0)
    lax.fori_loop(0, qi, body, 0)
    for j in range(nb):
        for hh in heads:
            if j + 1 < nb:
                scores(hh, nb * qi + j + 1, j + 1, q_lo=(j + 1) * tk)
            fold(hh, nb * qi + j, j, q_lo=j * tk, diagonal=True)
    for hh in heads:
        out_t = acc_ref[hh, 0:HEAD_DV, :] / acc_ref[hh, HEAD_DV:HEAD_DV + 1, :]
        o_ref[0, :, hh * FX_DH:(hh + 1) * FX_DH] = out_t.T.astype(o_ref.dtype)


def _fox(q, k, vt, fcum, *, tq, tk):
    _, B, S, _ = q.shape
    hp = FOX_HEADS_PER_STEP
    qspec = pl.BlockSpec((1, tq, hp * FX_DH), lambda b, h, i: (b, i, h))
    return pl.pallas_call(
        functools.partial(_fox_body, tq=tq, tk=tk),
        grid=(B, HEADS // hp, S // tq),
        in_specs=[pl.BlockSpec((1, 1, tq, hp * FX_DH), lambda b, h, i: (h, b, i, 0)),
                  pl.BlockSpec((1, 1, S, hp * FX_DH), lambda b, h, i: (h, b, 0, 0)),
                  pl.BlockSpec((1, hp * HEAD_DV, S), lambda b, h, i: (b, h, 0)),
                  pl.BlockSpec((1, hp, 1, S), lambda b, h, i: (b, h, 0, 0))],
        out_specs=qspec,
        out_shape=jax.ShapeDtypeStruct((B, S, BW), BF16),
        scratch_shapes=[pltpu.VMEM((hp, S, LANES), BF16), pltpu.VMEM((hp, tq // tk, tk, tq), F32),
                        pltpu.VMEM((hp, 1, tq), F32), pltpu.VMEM((hp, HEAD_DV + BF16_ROWS, tq), F32)],
        compiler_params=_cparams(("parallel", "parallel", "arbitrary")),
        name="fox_attn",
    )(q, k, vt, fcum)


def _mlstm_body(qt_ref, k_ref, vt_ref, sm_ref, smt_ref, o_ref, gn_ref, y_ref, ct_ref, n_ref, m_ref, *, L):
    @pl.when(pl.program_id(1) == 0)
    def _():
        ct_ref[...] = jnp.zeros_like(ct_ref)
        n_ref[...] = jnp.zeros_like(n_ref)
        m_ref[...] = jnp.zeros_like(m_ref)

    nbat = k_ref.shape[0]
    ri = lax.broadcasted_iota(jnp.int32, (L, L), 0)
    ci = lax.broadcasted_iota(jnp.int32, (L, L), 1)
    visible = ri <= ci
    tri_down = (ci <= ri).astype(F32)
    tri_right = visible.astype(F32)
    part_row = lax.broadcasted_iota(jnp.int32, (BF16_ROWS, ML_DK), 0)
    gates = []
    for bb in range(nbat):
        g = sm_ref[bb]
        gt = smt_ref[bb]
        b_cols = _dot(tri_down, _log_sigmoid(g), precision=HIGHEST)
        b_rows = _dot(_log_sigmoid(gt[0:SUBLANES, :]), tri_right, precision=HIGHEST)
        gates.append((g, gt, b_cols, b_rows))

    for bb, hd in [(bb, hd) for hd in range(HEADS) for bb in range(nbat)]:
        g, gt, b_cols, b_rows = gates[bb]
        sidx = bb * HEADS + hd
        sl = slice(hd * HEAD_DV, (hd + 1) * HEAD_DV)
        qt = qt_ref[bb, sl, :]
        k = k_ref[bb, :, sl]
        vt = vt_ref[bb, sl, :]
        b_row = b_rows[SM_MF + hd:SM_MF + hd + 1, :]
        li_row = gt[SM_MI + hd:SM_MI + hd + 1, :]
        u_col = g[:, SM_MI + hd:SM_MI + hd + 1] - b_cols[:, SM_MF + hd:SM_MF + hd + 1]
        b_last = b_row[:, L - 1:L]
        ct_prev = ct_ref[sidx]
        n_prev = n_ref[sidx]
        m_prev = m_ref[sidx][0:1, 0:1]

        dt = jnp.where(visible, b_row + u_col, -jnp.inf)
        inter = b_row + m_prev
        m_t = jnp.maximum(inter, jnp.max(dt, axis=0, keepdims=True))
        st = _dot(k, qt) * jnp.exp(dt - m_t)
        w_inter = jnp.exp(inter - m_t)
        num_t = _dot(vt, st.astype(BF16)) + w_inter * _dot(ct_prev.astype(BF16), qt)
        n_hi, n_mid, n_lo = (jnp.broadcast_to(part.astype(F32), (BF16_ROWS, ML_DK)) for part in _split3(n_prev))
        n_rows = jnp.where(part_row == 0, n_hi, jnp.where(part_row == 1, n_mid, jnp.where(part_row == 2, n_lo, 0.0)))
        qn = _dot(n_rows.astype(BF16), qt)
        den = jnp.sum(st, axis=0, keepdims=True) + w_inter * (qn[0:1, :] + qn[1:2, :] + qn[2:3, :])
        h_t = num_t * (1.0 / jnp.maximum(jnp.abs(den), jnp.exp(-m_t)))
        hn_t = h_t * lax.rsqrt(jnp.mean(h_t * h_t, axis=0, keepdims=True) + EPS)
        y = hn_t.T * gn_ref[:, sl] * _sigmoid(o_ref[bb, :, sl])
        y_ref[bb, :, sl] = y.astype(y_ref.dtype)

        m_loc = jnp.max(b_last - b_row + li_row, axis=-1, keepdims=True)
        m_new = jnp.maximum(b_last + m_prev, m_loc)
        a = jnp.exp(b_last + m_prev - m_new)
        ke = k.astype(F32) * jnp.exp(b_last + u_col - m_new)
        ct_ref[sidx] = a * ct_prev + _dot(vt, ke.astype(BF16))
        n_ref[sidx] = a * n_prev + jnp.sum(ke, axis=0, keepdims=True)
        m_ref[sidx] = jnp.broadcast_to(m_new, m_ref.shape[1:])


def _mlstm(qt, k, vt, small, small_t, o, gn, *, L):
    B, S, _ = k.shape
    nbat = ML_BATCH_PER_STEP if B % ML_BATCH_PER_STEP == 0 else 1
    blk = pl.BlockSpec((nbat, L, BW), lambda b, c: (b, c, 0))
    tblk = pl.BlockSpec((nbat, BW, L), lambda b, c: (b, 0, c))
    return pl.pallas_call(
        functools.partial(_mlstm_body, L=L),
        grid=(B // nbat, S // L),
        in_specs=[tblk, blk, tblk, pl.BlockSpec((nbat, L, SMALL_W), lambda b, c: (b, c, 0)),
                  pl.BlockSpec((nbat, SMALL_T_ROWS, L), lambda b, c: (b, 0, c)), blk,
                  pl.BlockSpec((1, BW), lambda b, c: (0, 0))],
        out_specs=blk,
        out_shape=jax.ShapeDtypeStruct((B, S, BW), BF16),
        scratch_shapes=[pltpu.VMEM((nbat * HEADS, HEAD_DV, ML_DK), F32),
                        pltpu.VMEM((nbat * HEADS, 1, ML_DK), F32),
                        pltpu.VMEM((nbat * HEADS, SUBLANES, LANES), F32)],
        compiler_params=_cparams(("parallel", "arbitrary")),
        name="mlstm",
    )(qt, k, vt, small, small_t, o, gn)


def _gelu_tanh(x):
    return 0.5 * x * (1.0 + jnp.tanh(0.7978845608028654 * (x + 0.044715 * (x * x * x))))


def _lru_body(x_ref, y_ref, cw_ref, cb_ref, wa_ref, ba_ref, wx_ref, bx_ref, lam_ref, o_ref,
              buf_ref, h_ref, *, ts):
    W = x_ref.shape[-1]
    pad = SUBLANES

    @pl.when(pl.program_id(1) == 0)
    def _():
        buf_ref[0:pad, :] = jnp.zeros((pad, W), F32)
        h_ref[...] = jnp.zeros_like(h_ref)

    x = x_ref[0]
    buf_ref[pad:pad + ts, :] = x
    xc = cb_ref[...] + cw_ref[CONV_WIDTH - 1:CONV_WIDTH, :] * x
    for j in range(CONV_WIDTH - 1):
        sh = CONV_WIDTH - 1 - j
        xc = xc + cw_ref[j:j + 1, :] * buf_ref[pad - sh:pad - sh + ts, :]
    buf_ref[0:pad, :] = x[ts - pad:ts, :]

    xb = xc.astype(BF16)
    r = _sigmoid(_dot(xb, wa_ref[...]) + ba_ref[...])
    ig = _sigmoid(_dot(xb, wx_ref[...]) + bx_ref[...])
    log_a = (-LRU_C) * r * _softplus(-lam_ref[...])
    a = jnp.exp(log_a)
    u = jnp.sqrt(-jnp.tanh(log_a) * (a * a + 1.0)) * (ig * xc)

    ng = ts // SUBLANES
    a = a.reshape(ng, SUBLANES, W)
    u = u.reshape(ng, SUBLANES, W)
    rid = lax.broadcasted_iota(jnp.int32, (ng, SUBLANES, W), 1)
    sh = 1
    while sh < SUBLANES:
        ok = rid >= sh
        a_s = jnp.where(ok, pltpu.roll(a, sh, axis=1), 1.0)
        u_s = jnp.where(ok, pltpu.roll(u, sh, axis=1), 0.0)
        u = u + a * u_s
        a = a * a_s
        sh *= 2

    carry = h_ref[...]
    rows = []
    for gidx in range(ng):
        hg = u[gidx] + a[gidx] * carry
        rows.append(hg)
        carry = hg[SUBLANES - 1:SUBLANES, :]
    h_ref[...] = carry
    hs = jnp.concatenate(rows, axis=0)
    o_ref[0] = (hs * _gelu_tanh(y_ref[0])).astype(o_ref.dtype)


def _lru(xb, yb, cw, cb, wa, ba, wx, bx, lam, *, ts):
    B, S, W = xb.shape
    blk = pl.BlockSpec((1, ts, W), lambda b, c: (b, c, 0))
    vec = pl.BlockSpec((1, W), lambda b, c: (0, 0))
    mat = pl.BlockSpec((W, W), lambda b, c: (0, 0))
    return pl.pallas_call(
        functools.partial(_lru_body, ts=ts),
        grid=(B, S // ts),
        in_specs=[blk, blk, pl.BlockSpec((CONV_WIDTH, W), lambda b, c: (0, 0)), vec, mat, vec, mat, vec, vec],
        out_specs=blk,
        out_shape=jax.ShapeDtypeStruct((B, S, W), BF16),
        scratch_shapes=[pltpu.VMEM((ts + SUBLANES, W), F32), pltpu.VMEM((1, W), F32)],
        compiler_params=_cparams(("parallel", "arbitrary")),
        name="rglru",
    )(xb, yb, cw, cb, wa, ba, wx, bx, lam)


def _gla_body(qk_ref, v_ref, g_ref, sm_ref, wal_ref, bal_ref, gn_ref, tril_ref, y_ref, s_ref, *, ts):
    C = GLA_CHUNK
    QW = HEADS * GLA_DK
    nchunk = ts // C
    nbat = qk_ref.shape[0]

    @pl.when(pl.program_id(1) == 0)
    def _():
        s_ref[...] = jnp.zeros_like(s_ref)

    lane = lax.broadcasted_iota(jnp.int32, (ts, LANES), 1)
    cr = lax.broadcasted_iota(jnp.int32, (C, C), 0)
    cc = lax.broadcasted_iota(jnp.int32, (C, C), 1)
    causal = cc <= cr
    zero = jnp.zeros((), BF16)
    q_h, kt_h, ke_h, dec_h = {}, {}, {}, {}
    for bb in range(nbat):
        la = _log_sigmoid(_dot(sm_ref[bb], wal_ref[...], precision=HIGHEST) + bal_ref[...]) / GLA_TAU
        parts = jnp.concatenate(_split3(la), axis=1)
        r3 = _dot(tril_ref[...], parts)
        bc = r3[:, 0:QW] + r3[:, QW:2 * QW] + r3[:, 2 * QW:3 * QW]
        b_last = jnp.concatenate(
            [jnp.broadcast_to(bc[(c + 1) * C - 1:(c + 1) * C, :], (C, QW)) for c in range(nchunk)], axis=0)
        qk = qk_ref[bb]
        q_t = (qk[:, 0:QW] * jnp.exp(bc)).astype(BF16)
        k = qk[:, QW:2 * QW]
        k_t = (k * jnp.exp(-bc)).astype(BF16)
        k_e = (k * jnp.exp(b_last - bc)).astype(BF16)
        dec = jnp.exp(b_last)
        for hd in range(HEADS):
            pair = slice((hd // 2) * LANES, (hd // 2 + 1) * LANES)
            mine = (lane >= GLA_DK) if hd % 2 else (lane < GLA_DK)
            q_h[bb, hd] = q_t[:, pair]
            kt_h[bb, hd] = jnp.where(mine, k_t[:, pair], zero)
            ke_h[bb, hd] = jnp.where(mine, k_e[:, pair], zero)
            dec_h[bb, hd] = dec[:, pair]
    chains = [(bb, hd) for bb in range(nbat) for hd in range(HEADS)]
    st = {ch: s_ref[ch[0] * HEADS + ch[1]] for ch in chains}
    for c in range(nchunk):
        rows = slice(c * C, (c + 1) * C)
        for ch in chains:
            bb, hd = ch
            vs = slice(hd * HEAD_DV, (hd + 1) * HEAD_DV)
            v = v_ref[bb, rows, vs]
            attn = jnp.where(causal, _dot_nt(q_h[ch][rows], kt_h[ch][rows]), 0.0)
            o = _dot(attn.astype(BF16), v) + _dot_nt(q_h[ch][rows], st[ch].astype(BF16))
            st[ch] = st[ch] * dec_h[ch][c * C:c * C + 1, :] + _dot_tn(v, ke_h[ch][rows])
            gg = g_ref[bb, rows, vs]
            y = _rms(o, gn_ref[:, vs]) * (gg * _sigmoid(gg))
            y_ref[bb, rows, vs] = y.astype(y_ref.dtype)
    for ch in chains:
        s_ref[ch[0] * HEADS + ch[1]] = st[ch]


def _gla(qk, v, g, small, wal, bal, gn, *, ts):
    B, S, _ = v.shape
    QW = HEADS * GLA_DK
    nbat = GLA_BATCH_PER_STEP if B % GLA_BATCH_PER_STEP == 0 else 1
    blk = pl.BlockSpec((nbat, ts, BW), lambda b, c: (b, c, 0))
    const = lambda b, c: (0, 0)
    pos = jnp.arange(ts, dtype=jnp.int32)
    same_chunk = (pos[:, None] // GLA_CHUNK) == (pos[None, :] // GLA_CHUNK)
    tril_blocks = (same_chunk & (pos[None, :] <= pos[:, None])).astype(BF16)
    return pl.pallas_call(
        functools.partial(_gla_body, ts=ts),
        grid=(B // nbat, S // ts),
        in_specs=[blk, blk, blk, pl.BlockSpec((nbat, ts, SMALL_W), lambda b, c: (b, c, 0)),
                  _resident((SMALL_W, QW), const), _resident((1, QW), const),
                  _resident((1, BW), const), _resident((ts, ts), const)],
        out_specs=blk,
        out_shape=jax.ShapeDtypeStruct((B, S, BW), BF16),
        scratch_shapes=[pltpu.VMEM((nbat * HEADS, HEAD_DV, LANES), F32)],
        compiler_params=_cparams(("parallel", "arbitrary")),
        name="gla",
    )(qk, v, g, small, wal, bal, gn, tril_blocks)


def _merge_body(x_ref, g1_ref, y0_ref, y1_ref, y2_ref, y3_ref, wg_ref, bg_ref, wb_ref, wo_ref, o_ref):
    x = x_ref[...]
    D = x.shape[-1]
    h = _rms(x, g1_ref[...]).astype(BF16)
    merged = None
    for n, y_ref in enumerate((y0_ref, y1_ref, y2_ref, y3_ref)):
        gate = _sigmoid(_dot(h, wg_ref[0, :, n * D:(n + 1) * D]) + bg_ref[:, n * D:(n + 1) * D])
        term = gate * _dot(y_ref[...], wb_ref[0, n])
        merged = term if merged is None else merged + term
    o_ref[...] = x + _dot(merged.astype(BF16), wo_ref[0])


def _merge(x2, g1, ys, wg, bg, wb, wo, *, layer, tm):
    T, D = x2.shape
    row = lambda i: (i, 0)
    const = lambda i: (0, 0)
    yspec = pl.BlockSpec((tm, BW), row)
    return pl.pallas_call(
        _merge_body,
        grid=(T // tm,),
        in_specs=[pl.BlockSpec((tm, D), row), _resident((1, D), const), yspec, yspec, yspec, yspec,
                  _resident((1, D, N_BRANCH * D), lambda i: (layer, 0, 0)), _resident((1, N_BRANCH * D), const),
                  _resident((1, N_BRANCH, BW, D), lambda i: (layer, 0, 0, 0)),
                  _resident((1, D, D), lambda i: (layer, 0, 0))],
        out_specs=pl.BlockSpec((tm, D), row),
        out_shape=jax.ShapeDtypeStruct((T, D), F32),
        compiler_params=_cparams(("parallel",)),
        name="merge",
    )(x2, g1, *ys, wg, bg, wb, wo)


def _mlp_body(x_ref, g2_ref, wu_ref, wd_ref, o_ref, *, nf):
    x = x_ref[...]
    h = _rms(x, g2_ref[...]).astype(BF16)
    dff = wu_ref.shape[2]
    fc = dff // nf
    acc = x
    for c in range(nf):
        up = jnp.maximum(_dot(h, wu_ref[0, :, c * fc:(c + 1) * fc]), 0.0)
        acc = acc + _dot((up * up).astype(BF16), wd_ref[0, c * fc:(c + 1) * fc, :])
    o_ref[...] = acc


def _mlp(x2, g2, wu, wd, *, layer, tm, nf=4):
    T, D = x2.shape
    dff = wu.shape[2]
    row = lambda i: (i, 0)
    const = lambda i: (0, 0)
    lay = lambda i: (layer, 0, 0)
    return pl.pallas_call(
        functools.partial(_mlp_body, nf=nf),
        grid=(T // tm,),
        in_specs=[pl.BlockSpec((tm, D), row), _resident((1, D), const),
                  _resident((1, D, dff), lay), _resident((1, dff, D), lay)],
        out_specs=pl.BlockSpec((tm, D), row),
        out_shape=jax.ShapeDtypeStruct((T, D), F32),
        compiler_params=_cparams(("parallel",)),
        name="mlp",
    )(x2, g2, wu, wd)


def _prep_bias(b_in, D):
    offs, widths = _in_offsets(D)
    take = lambda c: b_in[offs[c]:offs[c] + widths[c]]
    n_small = 3 * HEADS + GLA_RANK
    b_small = jnp.concatenate([take(C_MI), take(C_MF), take(C_FF), take(C_GA), jnp.zeros((SMALL_W - n_small,), F32)])
    b1 = jnp.concatenate([take(c) for c in W1_COLS] + [b_small])[None, :]
    bvt = jnp.concatenate([take(c) for c in WT_COLS])[:, None]
    bst = b_small[:SMALL_T_ROWS, None]
    bg = take(C_GATES)[None, :]
    cs = jnp.ones((1, BW), F32).at[:, 0:HEADS * GLA_DK].set(GLA_DK ** -0.5)
    return b1, bvt, bst, bg, cs


def _block_diag(w):
    nb, n, _ = w.shape
    eye = jnp.eye(nb, dtype=w.dtype)
    return (eye[:, None, :, None] * w[:, :, None, :]).reshape(nb * n, nb * n)


def _layer(x, p, w1, wvt, wst, wg, dense, layer, cfg):
    B, S, D = x.shape
    T = B * S
    x2 = x.reshape(T, D)
    b1, bvt, bst, bg, cs = _prep_bias(p["b_in"], D)
    g1 = p["norm1_g"][None, :]
    outs = _inproj(x2, g1, w1, b1, p["fx_qnorm_g"][None, :], p["fx_knorm_g"][None, :], cs, wvt, bvt, wst, bst,
                   layer=layer, tm=cfg["tm1"], seq=S)
    ml_qt, ml_vt, fx_vt, small_t = outs[N_GROUPS + 1:]
    (ml_k, ml_o, fx_q, fx_k, r_x, r_y, g_qk, g_v, g_g, small) = [
        o.reshape(o.shape[:-2] + (B, S, o.shape[-1])) for o in outs[:N_GROUPS + 1]]

    fcum = _fcum(small_t.reshape(B, SMALL_T_ROWS, S // LANES, LANES)).reshape(B, HEADS, 1, S)
    y_fx = _fox(fx_q, fx_k, fx_vt, fcum, tq=cfg["tq"], tk=cfg["tk"])

    y_ml = _mlstm(ml_qt, ml_k, ml_vt, small, small_t, ml_o, p["ml_norm_g"][None, :], L=cfg["ml_chunk"])

    y_lru = _lru(r_x, r_y, p["lru_conv_w"], p["lru_conv_b"][None, :],
                 _block_diag(p["lru_wa"]).astype(BF16), p["lru_ba"][None, :],
                 _block_diag(p["lru_wx"]).astype(BF16), p["lru_bx"][None, :],
                 p["lru_lambda"][None, :], ts=cfg["ts_lru"])

    wal = jnp.zeros((SMALL_W, HEADS * GLA_DK), F32).at[SM_GA:SM_GA + GLA_RANK, :].set(p["gla_w_alpha"])
    y_gla = _gla(g_qk, g_v, g_g, small, wal, p["gla_b_alpha"][None, :],
                 p["gla_norm_g"][None, :], ts=cfg["ts_gla"])

    ys = [y.reshape(T, BW) for y in (y_ml, y_fx, y_lru, y_gla)]
    x2 = _merge(x2, g1, ys, wg, bg, dense["w_branch"], dense["w_o"], layer=layer, tm=cfg["tm3"])
    x2 = _mlp(x2, p["norm2_g"][None, :], dense["w_up"], dense["w_down"], layer=layer, tm=cfg["tm4"])
    return x2.reshape(B, S, D)


def _config(S):
    return dict(tm1=512, tq=min(1024, S), tk=512, ml_chunk=min(256, S), ts_lru=min(512, S), ts_gla=min(512, S),
                tm3=512, tm4=512)


def kernel(x, norm1_g, w_in, b_in, ml_norm_g, fx_qnorm_g, fx_knorm_g, lru_conv_w, lru_conv_b, lru_wa, lru_ba,
           lru_wx, lru_bx, lru_lambda, gla_w_alpha, gla_b_alpha, gla_norm_g, w_branch, w_o, norm2_g, w_up,
           w_down):
    stacked = dict(norm1_g=norm1_g, b_in=b_in, ml_norm_g=ml_norm_g, fx_qnorm_g=fx_qnorm_g,
                   fx_knorm_g=fx_knorm_g, lru_conv_w=lru_conv_w, lru_conv_b=lru_conv_b, lru_wa=lru_wa,
                   lru_ba=lru_ba, lru_wx=lru_wx, lru_bx=lru_bx, lru_lambda=lru_lambda,
                   gla_w_alpha=gla_w_alpha, gla_b_alpha=gla_b_alpha, gla_norm_g=gla_norm_g, norm2_g=norm2_g)
    dense = dict(w_branch=w_branch.astype(BF16), w_o=w_o.astype(BF16), w_up=w_up.astype(BF16),
                 w_down=w_down.astype(BF16))
    cfg = _config(x.shape[1])
    w1, wvt, wst, wg = _wprep(w_in)
    for layer in range(norm1_g.shape[0]):
        x = _layer(x, {name: val[layer] for name, val in stacked.items()}, w1, wvt, wst, wg, dense, layer, cfg)
    return x
```

```python
import functools
import math

import jax
import jax.numpy as jnp
from jax import lax
from jax.experimental import pallas as pl
from jax.experimental.pallas import tpu as pltpu

F32 = jnp.float32
BF16 = jnp.bfloat16
HIGHEST = lax.Precision.HIGHEST

N_BRANCH = 4
BRANCH_WIDTH = 512
BW = BRANCH_WIDTH
HEADS = 4
ML_DK = 128
FX_DH = 128
HEAD_DV = 128
CONV_WIDTH = 4
LRU_C = 8.0
GLA_DK = 64
GLA_RANK = 16
GLA_TAU = 16.0
GLA_CHUNK = 64
FOX_HEADS_PER_STEP = 2
GLA_BATCH_PER_STEP = 4
ML_BATCH_PER_STEP = 4
EPS = 1e-6
LOG2E = math.log2(math.e)

LANES = 128
SUBLANES = 8
BF16_ROWS = 16
VMEM_LIMIT = 56 * 1024 * 1024

SM_MI, SM_MF, SM_FF, SM_GA = 0, 4, 8, 12
SMALL_W = LANES
SMALL_T_ROWS = 16
NEG = -1e30


def _cparams(sem):
    return pltpu.CompilerParams(dimension_semantics=sem, vmem_limit_bytes=VMEM_LIMIT)


def _resident(shape, index_map):
    return pl.BlockSpec(shape, index_map, pipeline_mode=pl.Buffered(1))


def _log_sigmoid(x):
    return jnp.minimum(x, 0.0) - jnp.log1p(jnp.exp(-jnp.abs(x)))


def _softplus(x):
    return jnp.maximum(x, 0.0) + jnp.log1p(jnp.exp(-jnp.abs(x)))


def _sigmoid(x):
    return 0.5 * jnp.tanh(0.5 * x) + 0.5


def _rms(x, g):
    return x * lax.rsqrt(jnp.mean(x * x, axis=-1, keepdims=True) + EPS) * g


def _dot(a, b, **kw):
    return jnp.dot(a, b, preferred_element_type=F32, **kw)


def _dot_nt(a, b, **kw):
    return lax.dot_general(a, b, (((1,), (1,)), ((), ())), preferred_element_type=F32, **kw)


def _dot_tn(a, b, **kw):
    return lax.dot_general(a, b, (((0,), (0,)), ((), ())), preferred_element_type=F32, **kw)


def _split3(x):
    hi = x.astype(BF16)
    r1 = x - hi.astype(F32)
    mid = r1.astype(BF16)
    lo = (r1 - mid.astype(F32)).astype(BF16)
    return hi, mid, lo


def _in_offsets(D):
    widths = (BW, BW, BW, BW, HEADS, HEADS, BW, BW, BW, HEADS, BW, BW,
              HEADS * GLA_DK, HEADS * GLA_DK, BW, BW, GLA_RANK, N_BRANCH * D)
    offs, acc = [], 0
    for w in widths:
        offs.append(acc)
        acc += w
    return offs, widths


(C_MQ, C_MK, C_MV, C_MO, C_MI, C_MF, C_FQ, C_FK, C_FV, C_FF, C_RX, C_RY,
 C_GQ, C_GK, C_GV, C_GG, C_GA, C_GATES) = range(18)
G_MLK, G_MLO, G_FXQ, G_FXK, G_RX, G_RY, G_GQK, G_GV, G_GG = range(9)
N_GROUPS = 9
K1_OUT_DTYPES = (BF16, F32, BF16, BF16, F32, F32, F32, BF16, F32)
W1_COLS = (C_MK, C_MO, C_FQ, C_FK, C_RX, C_RY, C_GQ, C_GK, C_GV, C_GG)
N1 = N_GROUPS * BW + SMALL_W
WT_COLS = (C_MQ, C_MV, C_FV)
NT = len(WT_COLS) * BW


def _wprep_body(w_ref, w1_ref, wvt_ref, wst_ref, wg_ref, *, D):
    offs, widths = _in_offsets(D)
    n_in = offs[-1] + widths[-1]

    def cols(c):
        a0 = (offs[c] // LANES) * LANES
        sh = offs[c] - a0
        end = min(a0 + widths[c] + (LANES if sh else 0), n_in)
        return w_ref[0, :, a0:end][:, sh:sh + widths[c]]

    dst = 0
    for c in W1_COLS:
        w1_ref[0, :, dst:dst + widths[c]] = cols(c).astype(BF16)
        dst += widths[c]

    def window(c):
        a0 = (offs[c] // LANES) * LANES
        return w_ref[0, :, a0:a0 + LANES]

    lane = lax.broadcasted_iota(jnp.int32, (w_ref.shape[1], LANES), 1)
    small = jnp.where(lane < SM_FF, window(C_MI),
                      jnp.where(lane < SM_GA, window(C_FF),
                                jnp.where(lane < SM_GA + GLA_RANK, window(C_GA), 0.0)))
    w1_ref[0, :, dst:dst + SMALL_W] = small.astype(BF16)
    wst_ref[0] = small.T[0:SMALL_T_ROWS, :].astype(BF16)
    for n, c in enumerate(WT_COLS):
        wvt_ref[0, n * BW:(n + 1) * BW, :] = cols(c).T.astype(BF16)
    wg_ref[0] = cols(C_GATES).astype(BF16)


def _wprep(w_in, *, rb=128):
    depth, D, n_in = w_in.shape
    offs, _ = _in_offsets(D)
    assert offs[C_MF] == offs[C_MI] + HEADS and offs[C_MI] % LANES == SM_MI
    assert offs[C_FF] % LANES == SM_FF and offs[C_GA] % LANES == SM_GA
    return pl.pallas_call(
        functools.partial(_wprep_body, D=D),
        grid=(depth, D // rb),
        in_specs=[pl.BlockSpec((1, rb, n_in), lambda l, i: (l, i, 0))],
        out_specs=[pl.BlockSpec((1, rb, N1), lambda l, i: (l, i, 0)),
                   pl.BlockSpec((1, NT, rb), lambda l, i: (l, 0, i)),
                   pl.BlockSpec((1, SMALL_T_ROWS, rb), lambda l, i: (l, 0, i)),
                   pl.BlockSpec((1, rb, N_BRANCH * D), lambda l, i: (l, i, 0))],
        out_shape=[jax.ShapeDtypeStruct((depth, D, N1), BF16),
                   jax.ShapeDtypeStruct((depth, NT, D), BF16),
                   jax.ShapeDtypeStruct((depth, SMALL_T_ROWS, D), BF16),
                   jax.ShapeDtypeStruct((depth, D, N_BRANCH * D), BF16)],
        compiler_params=_cparams(("parallel", "parallel")),
        name="wprep",
    )(w_in)


def _inproj_body(x_ref, g1_ref, w_ref, b_ref, gq_ref, gk_ref, cs_ref, wvt_ref, bvt_ref, wst_ref, bst_ref,
                 *out_refs):
    x = x_ref[...]
    h = _rms(x, g1_ref[...]).astype(BF16)

    def proj(off, width):
        return _dot(h, w_ref[0, :, off:off + width]) + b_ref[:, off:off + width]

    for gi in range(N_GROUPS):
        z = proj(gi * BW, BW)
        o_ref = out_refs[gi]
        if gi in (G_FXQ, G_FXK):
            g = gq_ref[...] if gi == G_FXQ else gk_ref[...]
            scale = FX_DH ** -0.5 * LOG2E if gi == G_FXQ else 1.0
            for hd in range(HEADS):
                zh = z[:, hd * FX_DH:(hd + 1) * FX_DH]
                grp, pos = divmod(hd, FOX_HEADS_PER_STEP)
                o_ref[grp, :, pos * FX_DH:(pos + 1) * FX_DH] = (_rms(zh, g) * scale).astype(o_ref.dtype)
        elif gi == G_GQK:
            o_ref[...] = (z * cs_ref[...]).astype(o_ref.dtype)
        else:
            o_ref[...] = z.astype(o_ref.dtype)
    out_refs[N_GROUPS][...] = proj(N_GROUPS * BW, SMALL_W)
    for n, scale in enumerate((ML_DK ** -0.5, 1.0, 1.0)):
        rows = slice(n * BW, (n + 1) * BW)
        zt = _dot_nt(wvt_ref[0, rows, :], h) + bvt_ref[rows, :]
        out_refs[N_GROUPS + 1 + n][0] = (zt * scale).astype(BF16)
    out_refs[N_GROUPS + 1 + len(WT_COLS)][0] = _dot_nt(wst_ref[0], h) + bst_ref[...]


def _inproj(x2, g1, w1, b1, gq, gk, cs, wvt, bvt, wst, bst, *, layer, tm, seq):
    T, D = x2.shape
    tps = seq // tm
    const = lambda i: (0, 0)
    lay = lambda i: (layer, 0, 0)
    row = lambda i: (i, 0)
    col = lambda i: (i // tps, 0, i % tps)
    out_shape = [jax.ShapeDtypeStruct((T, BW), dt) for dt in K1_OUT_DTYPES]
    out_specs = [pl.BlockSpec((tm, BW), row) for _ in K1_OUT_DTYPES]
    ngrp, gw = HEADS // FOX_HEADS_PER_STEP, FOX_HEADS_PER_STEP * FX_DH
    for gi in (G_FXQ, G_FXK):
        out_shape[gi] = jax.ShapeDtypeStruct((ngrp, T, gw), K1_OUT_DTYPES[gi])
        out_specs[gi] = pl.BlockSpec((ngrp, tm, gw), lambda i: (0, i, 0))
    out_shape += [jax.ShapeDtypeStruct((T, SMALL_W), F32)]
    out_shape += [jax.ShapeDtypeStruct((T // seq, BW, seq), BF16) for _ in WT_COLS]
    out_shape += [jax.ShapeDtypeStruct((T // seq, SMALL_T_ROWS, seq), F32)]
    out_specs += [pl.BlockSpec((tm, SMALL_W), row)]
    out_specs += [pl.BlockSpec((1, BW, tm), col) for _ in WT_COLS]
    out_specs += [pl.BlockSpec((1, SMALL_T_ROWS, tm), col)]
    return pl.pallas_call(
        _inproj_body,
        grid=(T // tm,),
        in_specs=[
            pl.BlockSpec((tm, D), row),
            _resident((1, D), const),
            _resident((1, D, N1), lay),
            _resident((1, N1), const),
            _resident((1, FX_DH), const),
            _resident((1, FX_DH), const),
            _resident((1, BW), const),
            _resident((1, NT, D), lay),
            _resident((NT, 1), const),
            _resident((1, SMALL_T_ROWS, D), lay),
            _resident((SMALL_T_ROWS, 1), const),
        ],
        out_specs=out_specs,
        out_shape=out_shape,
        compiler_params=_cparams(("parallel",)),
        name="inproj",
    )(x2, g1, w1, b1, gq, gk, cs, wvt, bvt, wst, bst)


def _fcum_body(f_ref, o_ref):
    ls = _log_sigmoid(f_ref[0, 0]) * LOG2E
    nb = ls.shape[0]
    ii = lax.broadcasted_iota(jnp.int32, (LANES, LANES), 0)
    jj = lax.broadcasted_iota(jnp.int32, (LANES, LANES), 1)
    local = _dot(ls, (ii <= jj).astype(F32), precision=HIGHEST)
    tot = jnp.broadcast_to(local[:, LANES - 1:LANES], (nb, LANES))
    ri = lax.broadcasted_iota(jnp.int32, (nb, nb), 0)
    ci = lax.broadcasted_iota(jnp.int32, (nb, nb), 1)
    offs = _dot((ci < ri).astype(F32), tot, precision=HIGHEST)
    o_ref[0, 0] = local + offs


def _fcum(small_t4):
    B, _, nb, _ = small_t4.shape
    return pl.pallas_call(
        _fcum_body, grid=(B, HEADS),
        in_specs=[pl.BlockSpec((1, 1, nb, LANES), lambda b, h: (b, SM_FF + h, 0, 0))],
        out_specs=pl.BlockSpec((1, 1, nb, LANES), lambda b, h: (b, h, 0, 0)),
        out_shape=jax.ShapeDtypeStruct((B, HEADS, nb, LANES), F32),
        compiler_params=_cparams(("parallel", "parallel")), name="fox_cumsum",
    )(small_t4)


def _fox_body(q_ref, k_ref, vt_ref, f_ref, o_ref, e_ref, st_ref, bm_ref, m_ref, acc_ref, *, tq, tk):
    qi = pl.program_id(2)
    S = e_ref.shape[1]
    nb = tq // tk
    heads = range(FOX_HEADS_PER_STEP)

    @pl.when(qi == 0)
    def _():
        rid = lax.broadcasted_iota(jnp.int32, (LANES, LANES), 0)

        def build(blk, carry):
            s0 = pl.multiple_of(blk * LANES, LANES)
            for hh in heads:
                hi, mid, lo = _split3(-f_ref[0, hh, :, pl.ds(s0, LANES)])
                bc = lambda part: jnp.broadcast_to(part.astype(F32), (LANES, LANES))
                parts = jnp.where(rid == 0, bc(hi), jnp.where(rid == 1, bc(mid), jnp.where(rid == 2, bc(lo), 0.0)))
                e_ref[hh, pl.ds(s0, LANES), :] = parts.T.astype(BF16)
            return carry

        lax.fori_loop(0, S // LANES, build, 0, unroll=2)

    lane = lax.broadcasted_iota(jnp.int32, (tq, LANES), 1)
    sel = jnp.where(lane < 3, 1.0, 0.0).astype(BF16)
    q_aug = [jnp.concatenate([q_ref[0, 0, :, hh * FX_DH:(hh + 1) * FX_DH], sel], axis=1) for hh in heads]
    ones_rows = jnp.ones((BF16_ROWS, tk), BF16)

    def scores(hh, kj, slot, q_lo=0, diagonal=False):
        k0 = pl.multiple_of(kj * tk, tk)
        k_aug = jnp.concatenate([k_ref[0, 0, pl.ds(k0, tk), hh * FX_DH:(hh + 1) * FX_DH],
                                 e_ref[hh, pl.ds(k0, tk), :]], axis=1)
        st = _dot_nt(k_aug, q_aug[hh][q_lo:, :])
        if diagonal:
            key = lax.broadcasted_iota(jnp.int32, st.shape, 0)
            qry = lax.broadcasted_iota(jnp.int32, st.shape, 1)
            st = jnp.where(key <= qry, st, NEG)
        st_ref[hh, slot, :, q_lo:] = st
        bm_ref[hh, slot, :, q_lo:] = jnp.max(st, axis=0, keepdims=True)

    def fold(hh, kj, slot, q_lo=0):
        st = st_ref[hh, slot, :, q_lo:]
        m_old = m_ref[hh, :, q_lo:]
        m_new = jnp.maximum(m_old, bm_ref[hh, slot, :, q_lo:])
        p = jnp.exp2(st - m_new).astype(BF16)
        k0 = pl.multiple_of(kj * tk, tk)
        v_aug = jnp.concatenate([vt_ref[0, hh * HEAD_DV:(hh + 1) * HEAD_DV, pl.ds(k0, tk)], ones_rows], axis=0)
        acc_ref[hh, :, q_lo:] = jnp.exp2(m_old - m_new) * acc_ref[hh, :, q_lo:] + _dot(v_aug, p)
        m_ref[hh, :, q_lo:] = m_new

    def body(i, carry):
        for j in range(nb):
            for hh in heads:
                scores(hh, nb * i + j + 1, (j + 1) % nb)
                fold(hh, nb * i + j, j)
        return carry

    m_ref[...] = jnp.full(m_ref.shape, NEG, F32)
    acc_ref[...] = jnp.zeros(acc_ref.shape, F32)
    for hh in heads:
        scores(hh, 0, 0)
    lax.fori_loop(0, qi, body, 0)
    key = lax.broadcasted_iota(jnp.int32, (tk, tq), 0)
    qry = lax.broadcasted_iota(jnp.int32, (tk, tq), 1)
    for hh in heads:
        st0 = jnp.where(key <= qry, st_ref[hh, 0], NEG)
        st_ref[hh, 0] = st0
        bm_ref[hh, 0] = jnp.max(st0, axis=0, keepdims=True)
    for j in range(nb):
        for hh in heads:
            if j + 1 < nb:
                scores(hh, nb * qi + j + 1, j + 1, q_lo=(j + 1) * tk, diagonal=True)
            fold(hh, nb * qi + j, j, q_lo=j * tk)
    for hh in heads:
        out_t = acc_ref[hh, 0:HEAD_DV, :] / acc_ref[hh, HEAD_DV:HEAD_DV + 1, :]
        o_ref[0, :, hh * FX_DH:(hh + 1) * FX_DH] = out_t.T.astype(o_ref.dtype)


def _fox(q, k, vt, fcum, *, tq, tk):
    _, B, S, _ = q.shape
    hp = FOX_HEADS_PER_STEP
    qspec = pl.BlockSpec((1, tq, hp * FX_DH), lambda b, h, i: (b, i, h))
    return pl.pallas_call(
        functools.partial(_fox_body, tq=tq, tk=tk),
        grid=(B, HEADS // hp, S // tq),
        in_specs=[pl.BlockSpec((1, 1, tq, hp * FX_DH), lambda b, h, i: (h, b, i, 0)),
                  pl.BlockSpec((1, 1, S, hp * FX_DH), lambda b, h, i: (h, b, 0, 0)),
                  pl.BlockSpec((1, hp * HEAD_DV, S), lambda b, h, i: (b, h, 0)),
                  pl.BlockSpec((1, hp, 1, S), lambda b, h, i: (b, h, 0, 0))],
        out_specs=qspec,
        out_shape=jax.ShapeDtypeStruct((B, S, BW), BF16),
        scratch_shapes=[pltpu.VMEM((hp, S, LANES), BF16), pltpu.VMEM((hp, tq // tk, tk, tq), F32),
                        pltpu.VMEM((hp, tq // tk, 1, tq), F32),
                        pltpu.VMEM((hp, 1, tq), F32), pltpu.VMEM((hp, HEAD_DV + BF16_ROWS, tq), F32)],
        compiler_params=_cparams(("parallel", "parallel", "arbitrary")),
        name="fox_attn",
    )(q, k, vt, fcum)


def _mlstm_body(qt_ref, k_ref, vt_ref, sm_ref, smt_ref, o_ref, gn_ref, y_ref, ct_ref, n_ref, m_ref, *, L):
    @pl.when(pl.program_id(1) == 0)
    def _():
        ct_ref[...] = jnp.zeros_like(ct_ref)
        n_ref[...] = jnp.zeros_like(n_ref)
        m_ref[...] = jnp.zeros_like(m_ref)

    nbat = k_ref.shape[0]
    ri = lax.broadcasted_iota(jnp.int32, (L, L), 0)
    ci = lax.broadcasted_iota(jnp.int32, (L, L), 1)
    visible = ri <= ci
    tri_down = (ci <= ri).astype(F32)
    tri_right = visible.astype(F32)
    part_row = lax.broadcasted_iota(jnp.int32, (BF16_ROWS, ML_DK), 0)
    gates = []
    for bb in range(nbat):
        g = sm_ref[bb]
        gt = smt_ref[bb]
        b_cols = _dot(tri_down, _log_sigmoid(g), precision=HIGHEST)
        b_rows = _dot(_log_sigmoid(gt[0:SUBLANES, :]), tri_right, precision=HIGHEST)
        gates.append((g, gt, b_cols, b_rows))

    for bb, hd in [(bb, hd) for hd in range(HEADS) for bb in range(nbat)]:
        g, gt, b_cols, b_rows = gates[bb]
        sidx = bb * HEADS + hd
        sl = slice(hd * HEAD_DV, (hd + 1) * HEAD_DV)
        qt = qt_ref[bb, sl, :]
        k = k_ref[bb, :, sl]
        vt = vt_ref[bb, sl, :]
        b_row = b_rows[SM_MF + hd:SM_MF + hd + 1, :]
        li_row = gt[SM_MI + hd:SM_MI + hd + 1, :]
        u_col = g[:, SM_MI + hd:SM_MI + hd + 1] - b_cols[:, SM_MF + hd:SM_MF + hd + 1]
        b_last = b_row[:, L - 1:L]
        ct_prev = ct_ref[sidx]
        n_prev = n_ref[sidx]
        m_prev = m_ref[sidx][0:1, 0:1]

        dt = jnp.where(visible, b_row + u_col, -jnp.inf)
        inter = b_row + m_prev
        m_t = jnp.maximum(inter, jnp.max(dt, axis=0, keepdims=True))
        st = _dot(k, qt) * jnp.exp(dt - m_t)
        w_inter = jnp.exp(inter - m_t)
        num_t = _dot(vt, st.astype(BF16)) + w_inter * _dot(ct_prev.astype(BF16), qt)
        n_hi, n_mid, n_lo = (jnp.broadcast_to(part.astype(F32), (BF16_ROWS, ML_DK)) for part in _split3(n_prev))
        n_rows = jnp.where(part_row == 0, n_hi, jnp.where(part_row == 1, n_mid, jnp.where(part_row == 2, n_lo, 0.0)))
        qn = _dot(n_rows.astype(BF16), qt)
        den = jnp.sum(st, axis=0, keepdims=True) + w_inter * (qn[0:1, :] + qn[1:2, :] + qn[2:3, :])
        h_t = num_t * (1.0 / jnp.maximum(jnp.abs(den), jnp.exp(-m_t)))
        hn_t = h_t * lax.rsqrt(jnp.mean(h_t * h_t, axis=0, keepdims=True) + EPS)
        y = hn_t.T * gn_ref[:, sl] * _sigmoid(o_ref[bb, :, sl])
        y_ref[bb, :, sl] = y.astype(y_ref.dtype)

        m_loc = jnp.max(b_last - b_row + li_row, axis=-1, keepdims=True)
        m_new = jnp.maximum(b_last + m_prev, m_loc)
        a = jnp.exp(b_last + m_prev - m_new)
        ke = k.astype(F32) * jnp.exp(b_last + u_col - m_new)
        ct_ref[sidx] = a * ct_prev + _dot(vt, ke.astype(BF16))
        n_ref[sidx] = a * n_prev + jnp.sum(ke, axis=0, keepdims=True)
        m_ref[sidx] = jnp.broadcast_to(m_new, m_ref.shape[1:])


def _mlstm(qt, k, vt, small, small_t, o, gn, *, L):
    B, S, _ = k.shape
    nbat = ML_BATCH_PER_STEP if B % ML_BATCH_PER_STEP == 0 else 1
    blk = pl.BlockSpec((nbat, L, BW), lambda b, c: (b, c, 0))
    tblk = pl.BlockSpec((nbat, BW, L), lambda b, c: (b, 0, c))
    return pl.pallas_call(
        functools.partial(_mlstm_body, L=L),
        grid=(B // nbat, S // L),
        in_specs=[tblk, blk, tblk, pl.BlockSpec((nbat, L, SMALL_W), lambda b, c: (b, c, 0)),
                  pl.BlockSpec((nbat, SMALL_T_ROWS, L), lambda b, c: (b, 0, c)), blk,
                  pl.BlockSpec((1, BW), lambda b, c: (0, 0))],
        out_specs=blk,
        out_shape=jax.ShapeDtypeStruct((B, S, BW), BF16),
        scratch_shapes=[pltpu.VMEM((nbat * HEADS, HEAD_DV, ML_DK), F32),
                        pltpu.VMEM((nbat * HEADS, 1, ML_DK), F32),
                        pltpu.VMEM((nbat * HEADS, SUBLANES, LANES), F32)],
        compiler_params=_cparams(("parallel", "arbitrary")),
        name="mlstm",
    )(qt, k, vt, small, small_t, o, gn)


def _gelu_tanh(x):
    return 0.5 * x * (1.0 + jnp.tanh(0.7978845608028654 * (x + 0.044715 * (x * x * x))))


def _lru_body(x_ref, y_ref, cw_ref, cb_ref, wa_ref, ba_ref, wx_ref, bx_ref, lam_ref, o_ref,
              buf_ref, h_ref, *, ts):
    W = x_ref.shape[-1]
    pad = SUBLANES

    @pl.when(pl.program_id(1) == 0)
    def _():
        buf_ref[0:pad, :] = jnp.zeros((pad, W), F32)
        h_ref[...] = jnp.zeros_like(h_ref)

    x = x_ref[0]
    buf_ref[pad:pad + ts, :] = x
    xc = cb_ref[...] + cw_ref[CONV_WIDTH - 1:CONV_WIDTH, :] * x
    for j in range(CONV_WIDTH - 1):
        sh = CONV_WIDTH - 1 - j
        xc = xc + cw_ref[j:j + 1, :] * buf_ref[pad - sh:pad - sh + ts, :]
    buf_ref[0:pad, :] = x[ts - pad:ts, :]

    xb = xc.astype(BF16)
    r = _sigmoid(_dot(xb, wa_ref[...]) + ba_ref[...])
    ig = _sigmoid(_dot(xb, wx_ref[...]) + bx_ref[...])
    log_a = (-LRU_C) * r * _softplus(-lam_ref[...])
    a = jnp.exp(log_a)
    u = jnp.sqrt(-jnp.tanh(log_a) * (a * a + 1.0)) * (ig * xc)

    ng = ts // SUBLANES
    a = a.reshape(ng, SUBLANES, W)
    u = u.reshape(ng, SUBLANES, W)
    rid = lax.broadcasted_iota(jnp.int32, (ng, SUBLANES, W), 1)
    sh = 1
    while sh < SUBLANES:
        ok = rid >= sh
        a_s = jnp.where(ok, pltpu.roll(a, sh, axis=1), 1.0)
        u_s = jnp.where(ok, pltpu.roll(u, sh, axis=1), 0.0)
        u = u + a * u_s
        a = a * a_s
        sh *= 2

    carry = h_ref[...]
    rows = []
    for gidx in range(ng):
        hg = u[gidx] + a[gidx] * carry
        rows.append(hg)
        carry = hg[SUBLANES - 1:SUBLANES, :]
    h_ref[...] = carry
    hs = jnp.concatenate(rows, axis=0)
    o_ref[0] = (hs * _gelu_tanh(y_ref[0])).astype(o_ref.dtype)


def _lru(xb, yb, cw, cb, wa, ba, wx, bx, lam, *, ts):
    B, S, W = xb.shape
    blk = pl.BlockSpec((1, ts, W), lambda b, c: (b, c, 0))
    vec = pl.BlockSpec((1, W), lambda b, c: (0, 0))
    mat = pl.BlockSpec((W, W), lambda b, c: (0, 0))
    return pl.pallas_call(
        functools.partial(_lru_body, ts=ts),
        grid=(B, S // ts),
        in_specs=[blk, blk, pl.BlockSpec((CONV_WIDTH, W), lambda b, c: (0, 0)), vec, mat, vec, mat, vec, vec],
        out_specs=blk,
        out_shape=jax.ShapeDtypeStruct((B, S, W), BF16),
        scratch_shapes=[pltpu.VMEM((ts + SUBLANES, W), F32), pltpu.VMEM((1, W), F32)],
        compiler_params=_cparams(("parallel", "arbitrary")),
        name="rglru",
    )(xb, yb, cw, cb, wa, ba, wx, bx, lam)


def _gla_body(qk_ref, v_ref, g_ref, sm_ref, wal_ref, bal_ref, gn_ref, tril_ref, y_ref, s_ref, *, ts):
    C = GLA_CHUNK
    QW = HEADS * GLA_DK
    nchunk = ts // C
    nbat = qk_ref.shape[0]

    @pl.when(pl.program_id(1) == 0)
    def _():
        s_ref[...] = jnp.zeros_like(s_ref)

    lane = lax.broadcasted_iota(jnp.int32, (ts, LANES), 1)
    cr = lax.broadcasted_iota(jnp.int32, (C, C), 0)
    cc = lax.broadcasted_iota(jnp.int32, (C, C), 1)
    causal = cc <= cr
    zero = jnp.zeros((), BF16)
    q_h, kt_h, ke_h, dec_h = {}, {}, {}, {}
    for bb in range(nbat):
        la = _log_sigmoid(_dot(sm_ref[bb], wal_ref[...], precision=HIGHEST) + bal_ref[...]) / GLA_TAU
        parts = jnp.concatenate(_split3(la), axis=1)
        r3 = _dot(tril_ref[...], parts)
        bc = r3[:, 0:QW] + r3[:, QW:2 * QW] + r3[:, 2 * QW:3 * QW]
        b_last = jnp.concatenate(
            [jnp.broadcast_to(bc[(c + 1) * C - 1:(c + 1) * C, :], (C, QW)) for c in range(nchunk)], axis=0)
        qk = qk_ref[bb]
        q_t = (qk[:, 0:QW] * jnp.exp(bc)).astype(BF16)
        k = qk[:, QW:2 * QW]
        k_t = (k * jnp.exp(-bc)).astype(BF16)
        k_e = (k * jnp.exp(b_last - bc)).astype(BF16)
        dec = jnp.exp(b_last)
        for hd in range(HEADS):
            pair = slice((hd // 2) * LANES, (hd // 2 + 1) * LANES)
            mine = (lane >= GLA_DK) if hd % 2 else (lane < GLA_DK)
            q_h[bb, hd] = q_t[:, pair]
            kt_h[bb, hd] = jnp.where(mine, k_t[:, pair], zero)
            ke_h[bb, hd] = jnp.where(mine, k_e[:, pair], zero)
            dec_h[bb, hd] = dec[:, pair]
    chains = [(bb, hd) for bb in range(nbat) for hd in range(HEADS)]
    st = {ch: s_ref[ch[0] * HEADS + ch[1]] for ch in chains}
    for c in range(nchunk):
        rows = slice(c * C, (c + 1) * C)
        for ch in chains:
            bb, hd = ch
            vs = slice(hd * HEAD_DV, (hd + 1) * HEAD_DV)
            v = v_ref[bb, rows, vs]
            attn = jnp.where(causal, _dot_nt(q_h[ch][rows], kt_h[ch][rows]), 0.0)
            o = _dot(attn.astype(BF16), v) + _dot_nt(q_h[ch][rows], st[ch].astype(BF16))
            st[ch] = st[ch] * dec_h[ch][c * C:c * C + 1, :] + _dot_tn(v, ke_h[ch][rows])
            gg = g_ref[bb, rows, vs]
            y = _rms(o, gn_ref[:, vs]) * (gg * _sigmoid(gg))
            y_ref[bb, rows, vs] = y.astype(y_ref.dtype)
    for ch in chains:
        s_ref[ch[0] * HEADS + ch[1]] = st[ch]


def _gla(qk, v, g, small, wal, bal, gn, *, ts):
    B, S, _ = v.shape
    QW = HEADS * GLA_DK
    nbat = GLA_BATCH_PER_STEP if B % GLA_BATCH_PER_STEP == 0 else 1
    blk = pl.BlockSpec((nbat, ts, BW), lambda b, c: (b, c, 0))
    const = lambda b, c: (0, 0)
    pos = jnp.arange(ts, dtype=jnp.int32)
    same_chunk = (pos[:, None] // GLA_CHUNK) == (pos[None, :] // GLA_CHUNK)
    tril_blocks = (same_chunk & (pos[None, :] <= pos[:, None])).astype(BF16)
    return pl.pallas_call(
        functools.partial(_gla_body, ts=ts),
        grid=(B // nbat, S // ts),
        in_specs=[blk, blk, blk, pl.BlockSpec((nbat, ts, SMALL_W), lambda b, c: (b, c, 0)),
                  _resident((SMALL_W, QW), const), _resident((1, QW), const),
                  _resident((1, BW), const), _resident((ts, ts), const)],
        out_specs=blk,
        out_shape=jax.ShapeDtypeStruct((B, S, BW), BF16),
        scratch_shapes=[pltpu.VMEM((nbat * HEADS, HEAD_DV, LANES), F32)],
        compiler_params=_cparams(("parallel", "arbitrary")),
        name="gla",
    )(qk, v, g, small, wal, bal, gn, tril_blocks)


def _merge_body(x_ref, g1_ref, y0_ref, y1_ref, y2_ref, y3_ref, wg_ref, bg_ref, wb_ref, wo_ref, o_ref):
    x = x_ref[...]
    D = x.shape[-1]
    h = _rms(x, g1_ref[...]).astype(BF16)
    merged = None
    for n, y_ref in enumerate((y0_ref, y1_ref, y2_ref, y3_ref)):
        gate = _sigmoid(_dot(h, wg_ref[0, :, n * D:(n + 1) * D]) + bg_ref[:, n * D:(n + 1) * D])
        term = gate * _dot(y_ref[...], wb_ref[0, n])
        merged = term if merged is None else merged + term
    o_ref[...] = x + _dot(merged.astype(BF16), wo_ref[0])


def _merge(x2, g1, ys, wg, bg, wb, wo, *, layer, tm):
    T, D = x2.shape
    row = lambda i: (i, 0)
    const = lambda i: (0, 0)
    yspec = pl.BlockSpec((tm, BW), row)
    return pl.pallas_call(
        _merge_body,
        grid=(T // tm,),
        in_specs=[pl.BlockSpec((tm, D), row), _resident((1, D), const), yspec, yspec, yspec, yspec,
                  _resident((1, D, N_BRANCH * D), lambda i: (layer, 0, 0)), _resident((1, N_BRANCH * D), const),
                  _resident((1, N_BRANCH, BW, D), lambda i: (layer, 0, 0, 0)),
                  _resident((1, D, D), lambda i: (layer, 0, 0))],
        out_specs=pl.BlockSpec((tm, D), row),
        out_shape=jax.ShapeDtypeStruct((T, D), F32),
        compiler_params=_cparams(("parallel",)),
        name="merge",
    )(x2, g1, *ys, wg, bg, wb, wo)


def _mlp_body(x_ref, g2_ref, wu_ref, wd_ref, o_ref, *, nf):
    x = x_ref[...]
    h = _rms(x, g2_ref[...]).astype(BF16)
    dff = wu_ref.shape[2]
    fc = dff // nf
    acc = x
    for c in range(nf):
        up = jnp.maximum(_dot(h, wu_ref[0, :, c * fc:(c + 1) * fc]), 0.0)
        acc = acc + _dot((up * up).astype(BF16), wd_ref[0, c * fc:(c + 1) * fc, :])
    o_ref[...] = acc


def _mlp(x2, g2, wu, wd, *, layer, tm, nf=4):
    T, D = x2.shape
    dff = wu.shape[2]
    row = lambda i: (i, 0)
    const = lambda i: (0, 0)
    lay = lambda i: (layer, 0, 0)
    return pl.pallas_call(
        functools.partial(_mlp_body, nf=nf),
        grid=(T // tm,),
        in_specs=[pl.BlockSpec((tm, D), row), _resident((1, D), const),
                  _resident((1, D, dff), lay), _resident((1, dff, D), lay)],
        out_specs=pl.BlockSpec((tm, D), row),
        out_shape=jax.ShapeDtypeStruct((T, D), F32),
        compiler_params=_cparams(("parallel",)),
        name="mlp",
    )(x2, g2, wu, wd)


def _prep_bias(b_in, D):
    offs, widths = _in_offsets(D)
    take = lambda c: b_in[offs[c]:offs[c] + widths[c]]
    n_small = 3 * HEADS + GLA_RANK
    b_small = jnp.concatenate([take(C_MI), take(C_MF), take(C_FF), take(C_GA), jnp.zeros((SMALL_W - n_small,), F32)])
    b1 = jnp.concatenate([take(c) for c in W1_COLS] + [b_small])[None, :]
    bvt = jnp.concatenate([take(c) for c in WT_COLS])[:, None]
    bst = b_small[:SMALL_T_ROWS, None]
    bg = take(C_GATES)[None, :]
    cs = jnp.ones((1, BW), F32).at[:, 0:HEADS * GLA_DK].set(GLA_DK ** -0.5)
    return b1, bvt, bst, bg, cs


def _block_diag(w):
    nb, n, _ = w.shape
    eye = jnp.eye(nb, dtype=w.dtype)
    return (eye[:, None, :, None] * w[:, :, None, :]).reshape(nb * n, nb * n)


def _layer(x, p, w1, wvt, wst, wg, dense, layer, cfg):
    B, S, D = x.shape
    T = B * S
    x2 = x.reshape(T, D)
    b1, bvt, bst, bg, cs = _prep_bias(p["b_in"], D)
    g1 = p["norm1_g"][None, :]
    outs = _inproj(x2, g1, w1, b1, p["fx_qnorm_g"][None, :], p["fx_knorm_g"][None, :], cs, wvt, bvt, wst, bst,
                   layer=layer, tm=cfg["tm1"], seq=S)
    ml_qt, ml_vt, fx_vt, small_t = outs[N_GROUPS + 1:]
    (ml_k, ml_o, fx_q, fx_k, r_x, r_y, g_qk, g_v, g_g, small) = [
        o.reshape(o.shape[:-2] + (B, S, o.shape[-1])) for o in outs[:N_GROUPS + 1]]

    fcum = _fcum(small_t.reshape(B, SMALL_T_ROWS, S // LANES, LANES)).reshape(B, HEADS, 1, S)
    y_fx = _fox(fx_q, fx_k, fx_vt, fcum, tq=cfg["tq"], tk=cfg["tk"])

    y_ml = _mlstm(ml_qt, ml_k, ml_vt, small, small_t, ml_o, p["ml_norm_g"][None, :], L=cfg["ml_chunk"])

    y_lru = _lru(r_x, r_y, p["lru_conv_w"], p["lru_conv_b"][None, :],
                 _block_diag(p["lru_wa"]).astype(BF16), p["lru_ba"][None, :],
                 _block_diag(p["lru_wx"]).astype(BF16), p["lru_bx"][None, :],
                 p["lru_lambda"][None, :], ts=cfg["ts_lru"])

    wal = jnp.zeros((SMALL_W, HEADS * GLA_DK), F32).at[SM_GA:SM_GA + GLA_RANK, :].set(p["gla_w_alpha"])
    y_gla = _gla(g_qk, g_v, g_g, small, wal, p["gla_b_alpha"][None, :],
                 p["gla_norm_g"][None, :], ts=cfg["ts_gla"])

    ys = [y.reshape(T, BW) for y in (y_ml, y_fx, y_lru, y_gla)]
    x2 = _merge(x2, g1, ys, wg, bg, dense["w_branch"], dense["w_o"], layer=layer, tm=cfg["tm3"])
    x2 = _mlp(x2, p["norm2_g"][None, :], dense["w_up"], dense["w_down"], layer=layer, tm=cfg["tm4"])
    return x2.reshape(B, S, D)


def _config(S):
    return dict(tm1=512, tq=min(1024, S), tk=512, ml_chunk=min(256, S), ts_lru=min(512, S), ts_gla=min(512, S),
                tm3=512, tm4=512)


def kernel(x, norm1_g, w_in, b_in, ml_norm_g, fx_qnorm_g, fx_knorm_g, lru_conv_w, lru_conv_b, lru_wa, lru_ba,
           lru_wx, lru_bx, lru_lambda, gla_w_alpha, gla_b_alpha, gla_norm_g, w_branch, w_o, norm2_g, w_up,
           w_down):
    stacked = dict(norm1_g=norm1_g, b_in=b_in, ml_norm_g=ml_norm_g, fx_qnorm_g=fx_qnorm_g,
                   fx_knorm_g=fx_knorm_g, lru_conv_w=lru_conv_w, lru_conv_b=lru_conv_b, lru_wa=lru_wa,
                   lru_ba=lru_ba, lru_wx=lru_wx, lru_bx=lru_bx, lru_lambda=lru_lambda,
                   gla_w_alpha=gla_w_alpha, gla_b_alpha=gla_b_alpha, gla_norm_g=gla_norm_g, norm2_g=norm2_g)
    dense = dict(w_branch=w_branch.astype(BF16), w_o=w_o.astype(BF16), w_up=w_up.astype(BF16),
                 w_down=w_down.astype(BF16))
    cfg = _config(x.shape[1])
    w1, wvt, wst, wg = _wprep(w_in)
    for layer in range(norm1_g.shape[0]):
        x = _layer(x, {name: val[layer] for name, val in stacked.items()}, w1, wvt, wst, wg, dense, layer, cfg)
    return x
```
